```python
import math
import jax, jax.numpy as jnp
from jax import lax
import numpy as np

D_MODEL = 1024
BATCH = 8
SEQ = 4096
DEPTH = 2
DEC_BATCH = 32
DEC_SEQ = 1
PAST_LEN = 16384
PAGE_SIZE = 128

H_RET = 8
DK_RET = 64
DV_RET = 64
D_RET_K = H_RET * DK_RET
D_RET_V = H_RET * DV_RET
RET_CHUNK = 128
D_CONV = 512
CONV_W = 3
H_FOX = 8
HD_FOX = 64
D_FOX = H_FOX * HD_FOX
Q_BLOCK = 128
N_BRANCH = 3
IN_COLS = 2 * D_RET_K + 2 * D_RET_V + 3 * D_CONV + 3 * D_FOX + H_FOX + N_BRANCH * D_MODEL
D_FF = 2816
N_EXPERTS = 8
TOP_K = 2
D_EXP = 3584
N_DENSE = (DEPTH + 1) // 2
N_MOE = DEPTH // 2
ROPE_BASE = 10000.0
EPS = 1e-6

kernel_name = "hybrid_retention_shortconv_fox_step"

F32 = jnp.float32


def _column_splits():
    sizes = (D_RET_K, D_RET_K, D_RET_V, D_RET_V, D_CONV, D_CONV, D_CONV,
             D_FOX, D_FOX, D_FOX, H_FOX, D_MODEL, D_MODEL, D_MODEL)
    out, acc = [], 0
    for s in sizes[:-1]:
        acc += s
        out.append(acc)
    return out


def rmsnorm(x, g):
    xf = x.astype(F32)
    y = xf * lax.rsqrt(jnp.mean(xf * xf, axis=-1, keepdims=True) + EPS)
    return (y * g.astype(F32)).astype(x.dtype)


def head_rmsnorm(x):
    xf = x.astype(F32)
    return xf * lax.rsqrt(jnp.mean(xf * xf, axis=-1, keepdims=True) + EPS)


def rope(x, pos):
    half = x.shape[-1] // 2
    inv = ROPE_BASE ** (-jnp.arange(half, dtype=F32) / half)
    ang = pos.astype(F32)[:, None] * inv[None, :]
    cos = jnp.cos(ang)[:, None, :]
    sin = jnp.sin(ang)[:, None, :]
    xf = x.astype(F32)
    x1, x2 = xf[..., :half], xf[..., half:]
    return jnp.concatenate([x1 * cos - x2 * sin, x2 * cos + x1 * sin], axis=-1).astype(x.dtype)


def retention_log_decay():
    return jnp.log(1.0 - jnp.exp2(-5.0 - jnp.arange(H_RET, dtype=F32)))


def retention_chunk(state, q, k, v, log_gamma):
    C = q.shape[1]
    i = jnp.arange(C, dtype=F32)
    diff = i[:, None] - i[None, :]
    decay = jnp.where(diff >= 0, jnp.exp(log_gamma[:, None, None] * jnp.maximum(diff, 0.0)), 0.0)
    scores = jnp.einsum('bihd,bjhd->bhij', q, k) * decay[None]
    inner = jnp.einsum('bhij,bjhe->bihe', scores, v)
    q_decay = jnp.exp(log_gamma[:, None] * (i + 1.0)[None, :]).T
    cross = jnp.einsum('bihd,bhde->bihe', q, state) * q_decay[None, :, :, None]
    k_decay = jnp.exp(log_gamma[:, None] * (C - 1.0 - i)[None, :]).T
    new_state = (jnp.exp(log_gamma * C)[None, :, None, None] * state
                 + jnp.einsum('bjhd,bjhe->bhde', k * k_decay[None, :, :, None], v))
    return new_state, inner + cross


def retention(q, k, v, state, log_gamma):
    B, L = q.shape[:2]
    C = RET_CHUNK if L % RET_CHUNK == 0 else L
    n = L // C

    def to_chunks(a):
        return jnp.moveaxis(a.astype(F32).reshape((B, n, C) + a.shape[2:]), 1, 0)

    def step(s, xs):
        return retention_chunk(s, xs[0], xs[1], xs[2], log_gamma)

    s_fin, o = lax.scan(step, state.astype(F32), (to_chunks(q), to_chunks(k), to_chunks(v)))
    return s_fin, jnp.moveaxis(o, 0, 1).reshape(B, L, H_RET, DV_RET)


def short_conv(u, buf, w):
    L = u.shape[1]
    full = jnp.concatenate([buf.astype(u.dtype), u], axis=1)
    y = full[:, 0:L] * w[0]
    for t in range(1, CONV_W):
        y = y + full[:, t:t + L] * w[t]
    return y, full[:, L:]


def fox_prompt(q, k, v, logf):
    B, L, H, hd = q.shape
    scale = hd ** -0.5
    c = jnp.cumsum(logf, axis=1)
    cT = c.transpose(0, 2, 1)
    kf = k.astype(F32)
    vf = v.astype(F32)
    nb = L // Q_BLOCK
    qb = jnp.moveaxis(q.astype(F32).reshape(B, nb, Q_BLOCK, H, hd), 1, 0)
    cb = jnp.moveaxis(c.reshape(B, nb, Q_BLOCK, H), 1, 0)
    starts = jnp.arange(nb) * Q_BLOCK
    kpos = jnp.arange(L)

    def block(args):
        qi, ci, s0 = args
        logits = jnp.einsum('bqhd,bkhd->bhqk', qi, kf) * scale
        logits = logits + ci.transpose(0, 2, 1)[..., None] - cT[:, :, None, :]
        qpos = s0 + jnp.arange(Q_BLOCK)
        mask = kpos[None, :] <= qpos[:, None]
        p = jax.nn.softmax(jnp.where(mask, logits, -jnp.inf), axis=-1)
        return jnp.einsum('bhqk,bkhd->bqhd', p, vf)

    out = lax.map(block, (qb, cb, starts))
    return jnp.moveaxis(out, 0, 1).reshape(B, L, H, hd)


def fox_decode(q, k, v, logf, k_past, v_past, logf_past):
    T = q.shape[1]
    P = k_past.shape[1]
    scale = HD_FOX ** -0.5
    qf = q.astype(F32)
    lfp = logf_past.astype(F32)
    r_past = lax.cumsum(lfp, axis=1, reverse=True) - lfp
    cT = jnp.cumsum(logf, axis=1).transpose(0, 2, 1)
    lp = (jnp.einsum('bqhd,bkhd->bhqk', qf, k_past.astype(F32)) * scale
          + cT[..., None] + r_past.transpose(0, 2, 1)[:, :, None, :])
    ln = jnp.einsum('bqhd,bkhd->bhqk', qf, k.astype(F32)) * scale + cT[..., None] - cT[:, :, None, :]
    causal = jnp.tril(jnp.ones((T, T), dtype=bool))
    ln = jnp.where(causal, ln, -jnp.inf)
    p = jax.nn.softmax(jnp.concatenate([lp, ln], axis=-1), axis=-1)
    return (jnp.einsum('bhqk,bkhd->bqhd', p[..., :P], v_past.astype(F32))
            + jnp.einsum('bhqk,bkhd->bqhd', p[..., P:], v.astype(F32)))


def token_mixer(h, pos, ret_state, conv_buf, fox_attend, log_gamma,
                w_in, b_forget, conv_w, w_ret_o, w_conv_o, w_fox_o, w_o):
    B, L, _ = h.shape
    z = h @ w_in
    (rq, rk, rv, rg, cb, cc, cx, fq, fk, fv, ff, a1, a2, a3) = jnp.split(z, _column_splits(), axis=-1)
    q = rope(rq.reshape(B, L, H_RET, DK_RET), pos)
    k = rope(rk.reshape(B, L, H_RET, DK_RET), pos) * (DK_RET ** -0.5)
    v = rv.reshape(B, L, H_RET, DV_RET)
    new_ret, o_ret = retention(q, k, v, ret_state, log_gamma)
    o_ret = head_rmsnorm(o_ret).reshape(B, L, D_RET_V).astype(h.dtype) * jax.nn.silu(rg)
    y_conv, new_buf = short_conv(cc * cx, conv_buf, conv_w)
    y_conv = cb * y_conv
    fq = fq.reshape(B, L, H_FOX, HD_FOX)
    fk = fk.reshape(B, L, H_FOX, HD_FOX)
    fv = fv.reshape(B, L, H_FOX, HD_FOX)
    logf = jax.nn.log_sigmoid(ff.astype(F32) + b_forget.astype(F32))
    o_fox = fox_attend(fq, fk, fv, logf).reshape(B, L, D_FOX).astype(h.dtype)
    merged = (jax.nn.sigmoid(a1) * (o_ret @ w_ret_o)
              + jax.nn.sigmoid(a2) * (y_conv @ w_conv_o)
              + jax.nn.sigmoid(a3) * (o_fox @ w_fox_o))
    return merged @ w_o, new_ret, new_buf, fk, fv, logf


def swiglu(x, wg, wu, wd):
    return (jax.nn.silu(x @ wg) * (x @ wu)) @ wd


def moe_swiglu(x, w_router, wg, wu, wd):
    probs = jax.nn.softmax((x @ w_router).astype(F32), axis=-1)
    top_p, top_i = lax.top_k(probs, TOP_K)
    top_p = top_p / jnp.sum(top_p, axis=-1, keepdims=True)
    combine = jnp.sum(jax.nn.one_hot(top_i, N_EXPERTS, dtype=F32) * top_p[..., None], axis=-2)
    out = jnp.zeros_like(x)
    for e in range(N_EXPERTS):
        out = out + combine[..., e:e + 1].astype(x.dtype) * swiglu(x, wg[e], wu[e], wd[e])
    return out


def channel_mixer(h, l, w_ffn_gate, w_ffn_up, w_ffn_down, w_router, w_exp_gate, w_exp_up, w_exp_down):
    m = l // 2
    if l % 2 == 0:
        return swiglu(h, w_ffn_gate[m], w_ffn_up[m], w_ffn_down[m])
    return moe_swiglu(h, w_router[m], w_exp_gate[m], w_exp_up[m], w_exp_down[m])


def setup_inputs(seed: int = 0) -> dict:
    key = jax.random.key(seed)
    ks = jax.random.split(key, 32)
    n_pages = PAST_LEN // PAGE_SIZE
    n_phys = (DEC_BATCH * n_pages * 5) // 4

    def nrm(k, shape, scale):
        return jax.random.normal(k, shape, F32) * scale

    def gain(k, shape):
        return 1.0 + nrm(k, shape, 0.02)

    forget_base = jnp.linspace(2.0, 6.0, H_FOX, dtype=F32)
    x_prompt = nrm(ks[0], (BATCH, SEQ, D_MODEL), 1.0)
    x_sample = nrm(ks[1], (DEC_BATCH, DEC_SEQ, D_MODEL), 1.0)
    state_ret = nrm(ks[2], (DEPTH, DEC_BATCH, H_RET, DK_RET, DV_RET), 0.5)
    state_conv = nrm(ks[3], (DEPTH, DEC_BATCH, CONV_W - 1, D_CONV), 1.0)
    cache_k = nrm(ks[4], (DEPTH, n_phys, PAGE_SIZE, H_FOX, HD_FOX), 1.0)
    cache_v = nrm(ks[5], (DEPTH, n_phys, PAGE_SIZE, H_FOX, HD_FOX), 1.0)
    cache_logf = jax.nn.log_sigmoid(forget_base + nrm(ks[6], (DEPTH, n_phys, PAGE_SIZE, H_FOX), 1.0))
    page_table = jax.random.permutation(ks[7], n_phys)[:DEC_BATCH * n_pages].reshape(
        DEC_BATCH, n_pages).astype(jnp.int32)
    return {
        "x_prompt": x_prompt,
        "x_sample": x_sample,
        "state_ret": state_ret,
        "state_conv": state_conv,
        "cache_k": cache_k,
        "cache_v": cache_v,
        "cache_logf": cache_logf,
        "page_table": page_table,
        "g_mix": gain(ks[8], (DEPTH, D_MODEL)),
        "w_in": nrm(ks[9], (DEPTH, D_MODEL, IN_COLS), D_MODEL ** -0.5),
        "b_forget": forget_base + nrm(ks[10], (DEPTH, H_FOX), 0.1),
        "conv_w": nrm(ks[11], (DEPTH, CONV_W, D_CONV), CONV_W ** -0.5),
        "w_ret_o": nrm(ks[12], (DEPTH, D_RET_V, D_MODEL), D_RET_V ** -0.5),
        "w_conv_o": nrm(ks[13], (DEPTH, D_CONV, D_MODEL), D_CONV ** -0.5),
        "w_fox_o": nrm(ks[14], (DEPTH, D_FOX, D_MODEL), D_FOX ** -0.5),
        "w_o": nrm(ks[15], (DEPTH, D_MODEL, D_MODEL), D_MODEL ** -0.5),
        "g_ffn": gain(ks[16], (DEPTH, D_MODEL)),
        "w_ffn_gate": nrm(ks[17], (N_DENSE, D_MODEL, D_FF), D_MODEL ** -0.5),
        "w_ffn_up": nrm(ks[18], (N_DENSE, D_MODEL, D_FF), D_MODEL ** -0.5),
        "w_ffn_down": nrm(ks[19], (N_DENSE, D_FF, D_MODEL), D_FF ** -0.5),
        "w_router": nrm(ks[20], (N_MOE, D_MODEL, N_EXPERTS), D_MODEL ** -0.5),
        "w_exp_gate": nrm(ks[21], (N_MOE, N_EXPERTS, D_MODEL, D_EXP), D_MODEL ** -0.5),
        "w_exp_up": nrm(ks[22], (N_MOE, N_EXPERTS, D_MODEL, D_EXP), D_MODEL ** -0.5),
        "w_exp_down": nrm(ks[23], (N_MOE, N_EXPERTS, D_EXP, D_MODEL), D_EXP ** -0.5),
        "g_final": gain(ks[24], (D_MODEL,)),
    }


def reference(x_prompt, x_sample, state_ret, state_conv, cache_k, cache_v, cache_logf, page_table,
              g_mix, w_in, b_forget, conv_w, w_ret_o, w_conv_o, w_fox_o, w_o,
              g_ffn, w_ffn_gate, w_ffn_up, w_ffn_down, w_router, w_exp_gate, w_exp_up, w_exp_down,
              g_final):
    Bp, L = x_prompt.shape[:2]
    Bd, T = x_sample.shape[:2]
    n_pages = page_table.shape[1]
    past_len = n_pages * PAGE_SIZE
    pos_p = jnp.arange(L)
    pos_s = past_len + jnp.arange(T)
    log_gamma = retention_log_decay()
    xp, xs = x_prompt, x_sample
    ret_p, ret_s, buf_p, buf_s = [], [], [], []
    kp_l, vp_l, lfp_l, ks_l, vs_l, lfs_l = [], [], [], [], [], []
    for l in range(DEPTH):
        mix_w = (w_in[l], b_forget[l], conv_w[l], w_ret_o[l], w_conv_o[l], w_fox_o[l], w_o[l])
        hp = rmsnorm(xp, g_mix[l])
        zero_ret = jnp.zeros((Bp, H_RET, DK_RET, DV_RET), F32)
        zero_buf = jnp.zeros((Bp, CONV_W - 1, D_CONV), xp.dtype)
        dp, sr, sb, fk, fv, lf = token_mixer(hp, pos_p, zero_ret, zero_buf, fox_prompt, log_gamma, *mix_w)
        xp = xp + dp
        ret_p.append(sr.astype(xp.dtype)); buf_p.append(sb)
        kp_l.append(fk); vp_l.append(fv); lfp_l.append(lf.astype(xp.dtype))
        hs = rmsnorm(xs, g_mix[l])
        k_past = cache_k[l, page_table].reshape(Bd, past_len, H_FOX, HD_FOX)
        v_past = cache_v[l, page_table].reshape(Bd, past_len, H_FOX, HD_FOX)
        lf_past = cache_logf[l, page_table].reshape(Bd, past_len, H_FOX)

        def fox_s(q, k, v, lf_new, k_past=k_past, v_past=v_past, lf_past=lf_past):
            return fox_decode(q, k, v, lf_new, k_past, v_past, lf_past)

        ds, sr2, sb2, fk2, fv2, lf2 = token_mixer(hs, pos_s, state_ret[l], state_conv[l], fox_s, log_gamma, *mix_w)
        xs = xs + ds
        ret_s.append(sr2.astype(state_ret.dtype)); buf_s.append(sb2)
        ks_l.append(fk2); vs_l.append(fv2); lfs_l.append(lf2.astype(xs.dtype))
        ffn_w = (w_ffn_gate, w_ffn_up, w_ffn_down, w_router, w_exp_gate, w_exp_up, w_exp_down)
        xp = xp + channel_mixer(rmsnorm(xp, g_ffn[l]), l, *ffn_w)
        xs = xs + channel_mixer(rmsnorm(xs, g_ffn[l]), l, *ffn_w)
    y_prompt = rmsnorm(xp, g_final)
    y_sample = rmsnorm(xs, g_final)
    return (y_prompt, y_sample,
            jnp.stack(ret_p), jnp.stack(ret_s),
            jnp.stack(buf_p), jnp.stack(buf_s),
            jnp.stack(kp_l), jnp.stack(vp_l), jnp.stack(lfp_l),
            jnp.stack(ks_l), jnp.stack(vs_l), jnp.stack(lfs_l))
```

```python
import functools

import jax
import jax.numpy as jnp
import numpy as np
from jax import lax
from jax.experimental import pallas as pl
from jax.experimental.pallas import tpu as pltpu

F32 = jnp.float32
BF16 = jnp.bfloat16

D_MODEL = 1024
H = 8
HD = 64
D_BR = H * HD
RET_CHUNK = 128
PAGE = 128
N_EXPERTS = 8
ROPE_BASE = 10000.0
EPS = 1e-6
LANE = 128
VMEM_LIMIT = 56 * 1024 * 1024

_RQ, _RK, _RV, _RG, _CB, _CC, _CX, _FQ, _FK, _FV = range(10)
Z_MAIN = 10 * D_BR
Z_FF = Z_MAIN + 3 * D_MODEL
Z_COLS = Z_FF + LANE
Z_TN = 640


def _cparams(sem):
    return pltpu.CompilerParams(dimension_semantics=sem, vmem_limit_bytes=VMEM_LIMIT)


def _rmsnorm(x, g):
    return x * lax.rsqrt(jnp.mean(x * x, axis=-1, keepdims=True) + EPS) * g


def _dot(a, b):
    return jnp.dot(a, b, preferred_element_type=F32)


def _dot_nt(a, b):
    return lax.dot_general(a, b, (((1,), (1,)), ((), ())), preferred_element_type=F32)


def _dot_tn(a, b):
    return lax.dot_general(a, b, (((0,), (0,)), ((), ())), preferred_element_type=F32)


def _dot_f32(a, b):
    return jnp.dot(a, b, preferred_element_type=F32, precision=lax.Precision.HIGHEST)


def _sigmoid(x):
    return 1.0 / (1.0 + jnp.exp(-x))


def _silu(x):
    return x * _sigmoid(x)


def _log_sigmoid(x):
    return jnp.minimum(x, 0.0) - jnp.log(1.0 + jnp.exp(-jnp.abs(x)))


def _const_spec(a):
    n = a.ndim
    return pl.BlockSpec(a.shape, lambda *_, n=n: (0,) * n)


def _inproj_kernel(x_ref, g_ref, w_ref, z_ref, h_scr):
    @pl.when(pl.program_id(1) == 0)
    def _():
        h_scr[...] = _rmsnorm(x_ref[...], g_ref[...]).astype(BF16)

    z_ref[...] = _dot(h_scr[...], w_ref[...])


def _inproj(x, g, w):
    m = x.shape[0]
    tm = min(m, 1024)
    return pl.pallas_call(
        _inproj_kernel,
        grid=(m // tm, Z_COLS // Z_TN),
        in_specs=[
            pl.BlockSpec((tm, D_MODEL), lambda i, j: (i, 0)),
            pl.BlockSpec((1, D_MODEL), lambda i, j: (0, 0)),
            pl.BlockSpec((D_MODEL, Z_TN), lambda i, j: (0, j)),
        ],
        out_specs=pl.BlockSpec((tm, Z_TN), lambda i, j: (i, j)),
        out_shape=jax.ShapeDtypeStruct((m, Z_COLS), F32),
        scratch_shapes=[pltpu.VMEM((tm, D_MODEL), BF16)],
        compiler_params=_cparams(("parallel", "arbitrary")),
        name="inproj",
    )(x, g, w)


def _rope(x, cos, sin_signed):
    lane = lax.broadcasted_iota(jnp.int32, x.shape, 1)
    first = (lane % HD) < (HD // 2)
    n = x.shape[1]
    partner = jnp.where(first, pltpu.roll(x, n - HD // 2, 1), pltpu.roll(x, HD // 2, 1))
    return x * cos + partner * sin_signed


def _ret_prompt_kernel(rq_ref, rk_ref, rv_ref, rg_ref, cos_ref, sin_ref, dec_ref, qd_ref, kd_ref, sd_ref,
                       grp_ref, o_ref, st_ref, s_scr):
    c = pl.program_id(1)

    @pl.when(c == 0)
    def _():
        s_scr[...] = jnp.zeros_like(s_scr)

    cos = cos_ref[...]
    sin = sin_ref[...]
    q = _rope(rq_ref[...], cos, sin)
    k = _rope(rk_ref[...], cos, sin) * (HD ** -0.5)
    v = rv_ref[...]
    low = lax.broadcasted_iota(jnp.int32, (RET_CHUNK, LANE), 1) < HD
    row = lax.broadcasted_iota(jnp.int32, (LANE, LANE), 0)
    col = lax.broadcasted_iota(jnp.int32, (LANE, LANE), 1)
    same_head = (row < HD) == (col < HD)
    for p in range(H // 2):
        sl = slice(p * LANE, (p + 1) * LANE)
        qp = q[:, sl]
        kp = k[:, sl].astype(BF16)
        vp = v[:, sl].astype(BF16)
        inner = []
        for hh in range(2):
            qm = jnp.where(low if hh == 0 else jnp.logical_not(low), qp, 0.0).astype(BF16)
            sc = _dot_nt(qm, kp) * dec_ref[2 * p + hh]
            inner.append(_dot(sc.astype(BF16), vp))
        inner = jnp.where(low, inner[0], inner[1])
        s_old = s_scr[p]
        cross = _dot(qp.astype(BF16), s_old.astype(BF16)) * qd_ref[:, sl]
        o = inner + cross
        kdec = (k[:, sl] * kd_ref[:, sl]).astype(BF16)
        upd = _dot_tn(kdec, vp)
        s_scr[p] = sd_ref[p] * s_old + jnp.where(same_head, upd, 0.0)
        ms = _dot_f32(o * o, grp_ref[...])
        o_ref[:, sl] = (o * lax.rsqrt(ms + EPS) * _silu(rg_ref[:, sl])).astype(BF16)

    @pl.when(c == pl.num_programs(1) - 1)
    def _():
        for p in range(H // 2):
            s = s_scr[p]
            st_ref[0, 2 * p] = s[:HD, :HD]
            st_ref[0, 2 * p + 1] = s[HD:, HD:]


def _ret_tables(log_gamma):
    i = jnp.arange(RET_CHUNK, dtype=F32)
    diff = i[:, None] - i[None, :]
    decay = jnp.where(diff >= 0, jnp.exp(log_gamma[:, None, None] * jnp.maximum(diff, 0.0)), 0.0)
    q_decay = jnp.exp(log_gamma[:, None] * (i + 1.0)[None, :]).T
    k_decay = jnp.exp(log_gamma[:, None] * (RET_CHUNK - 1.0 - i)[None, :]).T
    s_decay = jnp.exp(log_gamma * RET_CHUNK)
    qd = jnp.repeat(q_decay, HD, axis=1)
    kd = jnp.repeat(k_decay, HD, axis=1)
    sd = jnp.broadcast_to(jnp.repeat(s_decay, HD).reshape(H // 2, LANE, 1), (H // 2, LANE, LANE))
    lane = np.arange(LANE)
    grp = jnp.asarray(((lane[:, None] // HD) == (lane[None, :] // HD)).astype(np.float32) / HD)
    return decay, qd, kd, sd, grp


def _ret_prompt(z, cos, sin, tables, b, l):
    decay, qd, kd, sd, grp = tables
    nc = l // RET_CHUNK
    blk = lambda j: pl.BlockSpec((RET_CHUNK, D_BR), lambda bi, c, j=j: (bi * nc + c, j))
    tab = pl.BlockSpec((RET_CHUNK, D_BR), lambda bi, c: (c, 0))
    return pl.pallas_call(
        _ret_prompt_kernel,
        grid=(b, nc),
        in_specs=[blk(_RQ), blk(_RK), blk(_RV), blk(_RG), tab, tab,
                  _const_spec(decay), _const_spec(qd), _const_spec(kd), _const_spec(sd), _const_spec(grp)],
        out_specs=[pl.BlockSpec((RET_CHUNK, D_BR), lambda bi, c: (bi * nc + c, 0)),
                   pl.BlockSpec((1, H, HD, HD), lambda bi, c: (bi, 0, 0, 0))],
        out_shape=[jax.ShapeDtypeStruct((b * l, D_BR), BF16),
                   jax.ShapeDtypeStruct((b, H, HD, HD), F32)],
        scratch_shapes=[pltpu.VMEM((H // 2, LANE, LANE), F32)],
        compiler_params=_cparams(("parallel", "arbitrary")),
        name="ret_prompt",
    )(z, z, z, z, cos, sin, decay, qd, kd, sd, grp)


def _ret_step_kernel(q_ref, k_ref, v_ref, g_ref, s_ref, cos_ref, sin_ref, gam_ref, grow_ref, o_ref, sn_ref):
    cos = cos_ref[...]
    sin = sin_ref[...]
    q = _rope(q_ref[0], cos, sin).astype(BF16)
    k = (_rope(k_ref[0], cos, sin) * (HD ** -0.5)).astype(BF16)
    own = (lax.broadcasted_iota(jnp.int32, (H, D_BR), 1) // HD) == lax.broadcasted_iota(jnp.int32, (H, D_BR), 0)
    q_bd = jnp.where(own, jnp.broadcast_to(q.astype(F32), (H, D_BR)), 0.0)
    k_bd = jnp.where(own, jnp.broadcast_to(k.astype(F32), (H, D_BR)), 0.0)
    v = v_ref[0].astype(BF16)
    sc = jnp.sum(q_bd * k.astype(F32), axis=-1, keepdims=True)
    inner = sc.astype(BF16).astype(F32) * v.astype(F32)
    s_old = s_ref[0]
    cross = _dot(q_bd.astype(BF16), s_old.astype(BF16)) * gam_ref[...]
    o = inner + cross
    o = o * lax.rsqrt(jnp.mean(o * o, axis=-1, keepdims=True) + EPS)
    o_ref[0] = o * _silu(g_ref[0])
    sn_ref[0] = grow_ref[...] * s_old + _dot_tn(k_bd.astype(BF16), v)


def _ret_step(zs, state, cos512, sin512, gamma):
    bd = zs.shape[0]
    seg = lambda j: zs[:, j * D_BR:(j + 1) * D_BR]
    s2 = state.reshape(bd, D_BR, HD)
    gam = gamma.reshape(H, 1)
    grow = jnp.broadcast_to(jnp.repeat(gamma, HD)[:, None], (D_BR, HD))
    flat = pl.BlockSpec((1, 1, D_BR), lambda i: (i, 0, 0))
    r3 = pl.BlockSpec((1, H, HD), lambda i: (i, 0, 0))
    s3 = pl.BlockSpec((1, D_BR, HD), lambda i: (i, 0, 0))
    o, sn = pl.pallas_call(
        _ret_step_kernel,
        grid=(bd,),
        in_specs=[flat, flat, r3, r3, s3,
                  _const_spec(cos512), _const_spec(sin512), _const_spec(gam), _const_spec(grow)],
        out_specs=[r3, s3],
        out_shape=[jax.ShapeDtypeStruct((bd, H, HD), F32),
                   jax.ShapeDtypeStruct((bd, D_BR, HD), F32)],
        compiler_params=_cparams(("parallel",)),
        name="ret_step",
    )(seg(_RQ).reshape(bd, 1, D_BR), seg(_RK).reshape(bd, 1, D_BR), seg(_RV).reshape(bd, H, HD),
      seg(_RG).reshape(bd, H, HD), s2, cos512, sin512, gam, grow)
    return o.reshape(bd, D_BR), sn.reshape(bd, H, HD, HD)


def _forget_prompt_kernel(ff_ref, b_ref, tri_ref, exp_ref, lf_ref, ccol_ref, crow_ref, carry):
    @pl.when(pl.program_id(1) == 0)
    def _():
        carry[...] = jnp.zeros_like(carry)

    lf = _log_sigmoid(ff_ref[...] + b_ref[...])
    lf_ref[...] = lf[:, :H]
    c = _dot_f32(tri_ref[...], lf) + carry[...]
    ccol_ref[...] = _dot_f32(c, exp_ref[...])
    crow_ref[0] = c.T[:H, :]
    carry[...] = c[-1:, :]


def _forget_prompt(z, b_pad, b, l):
    tc = min(l, 512)
    nt = l // tc
    tri = jnp.asarray(np.tril(np.ones((tc, tc), np.float32)))
    lane = np.arange(D_BR)
    expand = np.zeros((LANE, D_BR), np.float32)
    expand[lane // HD, lane] = 1.0
    expand = jnp.asarray(expand)
    return pl.pallas_call(
        _forget_prompt_kernel,
        grid=(b, nt),
        in_specs=[pl.BlockSpec((tc, LANE), lambda bi, t: (bi * nt + t, Z_FF // LANE)),
                  _const_spec(b_pad), _const_spec(tri), _const_spec(expand)],
        out_specs=[pl.BlockSpec((tc, H), lambda bi, t: (bi * nt + t, 0)),
                   pl.BlockSpec((tc, D_BR), lambda bi, t: (bi * nt + t, 0)),
                   pl.BlockSpec((1, H, tc), lambda bi, t: (bi, 0, t))],
        out_shape=[jax.ShapeDtypeStruct((b * l, H), F32),
                   jax.ShapeDtypeStruct((b * l, D_BR), F32),
                   jax.ShapeDtypeStruct((b, H, l), F32)],
        scratch_shapes=[pltpu.VMEM((1, LANE), F32)],
        compiler_params=_cparams(("parallel", "arbitrary")),
        name="forget_prompt",
    )(z, b_pad, tri, expand)


def _fox_prompt_kernel(q_ref, k_ref, v_ref, ccol_ref, crow_ref, o_ref, *, tq, tk):
    qi = pl.program_id(2)
    q = q_ref[...] * (HD ** -0.5)
    low = lax.broadcasted_iota(jnp.int32, (tq, LANE), 1) < HD
    qpos = qi * tq + lax.broadcasted_iota(jnp.int32, (tq, tk), 0)
    koff = lax.broadcasted_iota(jnp.int32, (tq, tk), 1)
    nk = (qi * tq + tq + tk - 1) // tk
    outs = []
    for hh in range(2):
        qm = jnp.where(low if hh == 0 else jnp.logical_not(low), q, 0.0).astype(BF16)
        ccol = ccol_ref[:, hh * HD:hh * HD + 1]

        def body(j, carry, qm=qm, ccol=ccol, hh=hh):
            m, l, acc = carry
            ks = pl.multiple_of(j * tk, tk)
            kb = k_ref[pl.ds(ks, tk), :].astype(BF16)
            vb = v_ref[pl.ds(ks, tk), :].astype(BF16)
            s = _dot_nt(qm, kb) + ccol - crow_ref[0, 0, pl.ds(hh, 1), pl.ds(ks, tk)]
            s = jnp.where(koff + ks <= qpos, s, -jnp.inf)
            m_new = jnp.maximum(m, jnp.max(s, axis=-1, keepdims=True))
            alpha = jnp.exp(m - m_new)
            p = jnp.exp(s - m_new)
            l = alpha * l + jnp.sum(p, axis=-1, keepdims=True)
            acc = alpha * acc + _dot(p.astype(BF16), vb)
            return m_new, l, acc

        init = (jnp.full((tq, 1), -jnp.inf, F32), jnp.zeros((tq, 1), F32), jnp.zeros((tq, LANE), F32))
        m, l, acc = lax.fori_loop(0, nk, body, init)
        outs.append(acc / l)
    o_ref[...] = jnp.where(low, outs[0], outs[1]).astype(BF16)


def _fox_prompt(z, ccol, crow, b, l):
    tq = min(l, 256)
    tk = min(l, 256)
    nq = l // tq
    npair = H // 2
    crow4 = crow.reshape(b, npair, 2, l)
    qspec = lambda j: pl.BlockSpec((tq, LANE), lambda bi, p, qi, j=j: (bi * nq + qi, j * npair + p))
    kvspec = lambda j: pl.BlockSpec((l, LANE), lambda bi, p, qi, j=j: (bi, j * npair + p))
    return pl.pallas_call(
        functools.partial(_fox_prompt_kernel, tq=tq, tk=tk),
        grid=(b, npair, nq),
        in_specs=[qspec(_FQ), kvspec(_FK), kvspec(_FV),
                  pl.BlockSpec((tq, LANE), lambda bi, p, qi: (bi * nq + qi, p)),
                  pl.BlockSpec((1, 1, 2, l), lambda bi, p, qi: (bi, p, 0, 0))],
        out_specs=pl.BlockSpec((tq, LANE), lambda bi, p, qi: (bi * nq + qi, p)),
        out_shape=jax.ShapeDtypeStruct((b * l, D_BR), BF16),
        compiler_params=_cparams(("parallel", "parallel", "arbitrary")),
        name="fox_prompt",
    )(z, z, z, ccol, crow4)


def _fox_decode_kernel(pt_ref, q_ref, kn_ref, vn_ref, lfn_ref, suf_ref, *rest, g):
    k_refs = rest[:g]
    v_refs = rest[g:2 * g]
    lf_refs = rest[2 * g:3 * g]
    o_ref = rest[3 * g]
    m_scr, l_scr, acc_scr, carry = rest[3 * g + 1:]
    j = pl.program_id(1)
    own = (lax.broadcasted_iota(jnp.int32, (H, D_BR), 1) // HD) == lax.broadcasted_iota(jnp.int32, (H, D_BR), 0)
    q_bd = jnp.where(own, jnp.broadcast_to(q_ref[0] * (HD ** -0.5), (H, D_BR)), 0.0).astype(BF16)

    @pl.when(j == 0)
    def _():
        kn = kn_ref[0].astype(BF16).astype(F32)
        m_scr[...] = jnp.sum(q_bd.astype(F32) * kn, axis=-1, keepdims=True)
        l_scr[...] = jnp.ones_like(l_scr)
        acc_scr[...] = jnp.broadcast_to(vn_ref[0].astype(BF16).astype(F32), (H, D_BR))
        carry[...] = lfn_ref[0]

    m = m_scr[...]
    l = l_scr[...]
    acc = acc_scr[...]
    cy = carry[...]
    for t in range(g):
        kb = k_refs[t][0, 0].astype(BF16)
        vb = v_refs[t][0, 0].astype(BF16)
        lf = lf_refs[t][0, 0]
        s = _dot_nt(q_bd, kb) + _dot_f32(lf, suf_ref[...]) + cy
        cy = cy + jnp.sum(lf, axis=-1, keepdims=True)
        m_new = jnp.maximum(m, jnp.max(s, axis=-1, keepdims=True))
        alpha = jnp.exp(m - m_new)
        p = jnp.exp(s - m_new)
        l = alpha * l + jnp.sum(p, axis=-1, keepdims=True)
        acc = alpha * acc + _dot(p.astype(BF16), vb)
        m = m_new
    m_scr[...] = m
    l_scr[...] = l
    acc_scr[...] = acc
    carry[...] = cy

    @pl.when(j == pl.num_programs(1) - 1)
    def _():
        o_ref[0] = jnp.sum(jnp.where(own, acc / l, 0.0), axis=0, keepdims=True)


def _fox_decode(page_table, q, k_new, v_new, lf_new, cache_k, cache_v, cache_lf_t, layer):
    bd, n_pages = page_table.shape
    g = 8 if n_pages % 8 == 0 else 1
    steps = n_pages // g
    jj = np.arange(PAGE)
    suf = jnp.asarray((jj[:, None] > jj[None, :]).astype(np.float32))

    def page(t):
        return lambda b, j, pt, t=t: (layer, pt[b, n_pages - 1 - (j * g + t)], 0, 0)

    row = pl.BlockSpec((1, 1, D_BR), lambda b, j, pt: (b, 0, 0))
    in_specs = [row, row, row,
                pl.BlockSpec((1, H, 1), lambda b, j, pt: (b, 0, 0)),
                pl.BlockSpec((PAGE, PAGE), lambda b, j, pt: (0, 0))]
    in_specs += [pl.BlockSpec((1, 1, PAGE, D_BR), page(t)) for t in range(g)]
    in_specs += [pl.BlockSpec((1, 1, PAGE, D_BR), page(t)) for t in range(g)]
    in_specs += [pl.BlockSpec((1, 1, H, PAGE), page(t)) for t in range(g)]
    grid_spec = pltpu.PrefetchScalarGridSpec(
        num_scalar_prefetch=1,
        grid=(bd, steps),
        in_specs=in_specs,
        out_specs=pl.BlockSpec((1, 1, D_BR), lambda b, j, pt: (b, 0, 0)),
        scratch_shapes=[pltpu.VMEM((H, 1), F32), pltpu.VMEM((H, 1), F32),
                        pltpu.VMEM((H, D_BR), F32), pltpu.VMEM((H, 1), F32)],
    )
    out = pl.pallas_call(
        functools.partial(_fox_decode_kernel, g=g),
        grid_spec=grid_spec,
        out_shape=jax.ShapeDtypeStruct((bd, 1, D_BR), F32),
        compiler_params=_cparams(("parallel", "arbitrary")),
        name="fox_decode",
    )(page_table, q.reshape(bd, 1, D_BR), k_new.reshape(bd, 1, D_BR), v_new.reshape(bd, 1, D_BR),
      lf_new.reshape(bd, H, 1), suf, *([cache_k] * g), *([cache_v] * g), *([cache_lf_t] * g))
    return out.reshape(bd, D_BR)


def _merge_math(x, o_ret, y_conv, o_fox, a1, a2, a3, wr_ref, wc_ref, wf_ref, wo_ref):
    merged = (_sigmoid(a1) * _dot(o_ret, wr_ref[...])
              + _sigmoid(a2) * _dot(y_conv, wc_ref[...])
              + _sigmoid(a3) * _dot(o_fox, wf_ref[...]))
    return x + _dot(merged.astype(BF16), wo_ref[...])


def _merge_prompt_kernel(x_ref, oret_ref, ofox_ref, cb_ref, cc_ref, cx_ref, ccp_ref, cxp_ref,
                         a1_ref, a2_ref, a3_ref, cw_ref, wr_ref, wc_ref, wf_ref, wo_ref,
                         xo_ref, buf_ref, *, tm, tiles_per_seq):
    i = pl.program_id(0)
    u = cc_ref[...] * cx_ref[...]
    prev = ccp_ref[...] * cxp_ref[...]
    prev = jnp.where(i % tiles_per_seq == 0, 0.0, prev)
    p1 = prev[7:8, :]
    p2 = prev[6:7, :]
    row = lax.broadcasted_iota(jnp.int32, u.shape, 0)
    u1 = jnp.where(row >= 1, pltpu.roll(u, 1, 0), p1)
    u2 = jnp.where(row >= 2, pltpu.roll(u, 2, 0), jnp.where(row == 1, p1, p2))
    y = u2 * cw_ref[0:1, :] + u1 * cw_ref[1:2, :] + u * cw_ref[2:3, :]
    y_conv = (cb_ref[...] * y).astype(BF16)
    buf_ref[0] = u[tm - 2:, :]
    xo_ref[...] = _merge_math(x_ref[...], oret_ref[...], y_conv, ofox_ref[...],
                              a1_ref[...], a2_ref[...], a3_ref[...], wr_ref, wc_ref, wf_ref, wo_ref)


def _merge_prompt(x, z, o_ret, o_fox, conv_w, w_ret_o, w_conv_o, w_fox_o, w_o, b, l):
    m = b * l
    tm = min(l, 256)
    tps = l // tm
    zb = lambda j: pl.BlockSpec((tm, D_BR), lambda i, j=j: (i, j))
    zprev = lambda j: pl.BlockSpec((8, D_BR), lambda i, j=j: (jnp.maximum(i * (tm // 8) - 1, 0), j))
    za = lambda j: pl.BlockSpec((tm, D_MODEL), lambda i, j=j: (i, Z_MAIN // D_MODEL + j))
    rows512 = pl.BlockSpec((tm, D_BR), lambda i: (i, 0))
    rows = pl.BlockSpec((tm, D_MODEL), lambda i: (i, 0))
    return pl.pallas_call(
        functools.partial(_merge_prompt_kernel, tm=tm, tiles_per_seq=tps),
        grid=(m // tm,),
        in_specs=[rows, rows512, rows512, zb(_CB), zb(_CC), zb(_CX), zprev(_CC), zprev(_CX),
                  za(0), za(1), za(2), _const_spec(conv_w),
                  _const_spec(w_ret_o), _const_spec(w_conv_o), _const_spec(w_fox_o), _const_spec(w_o)],
        out_specs=[rows, pl.BlockSpec((1, 2, D_BR), lambda i: (i // tps, 0, 0))],
        out_shape=[jax.ShapeDtypeStruct((m, D_MODEL), F32),
                   jax.ShapeDtypeStruct((b, 2, D_BR), F32)],
        compiler_params=_cparams(("arbitrary",)),
        name="merge_prompt",
    )(x, o_ret, o_fox, z, z, z, z, z, z, z, z, conv_w, w_ret_o, w_conv_o, w_fox_o, w_o)


def _merge_step_kernel(x_ref, oret_ref, ofox_ref, cb_ref, cc_ref, cx_ref, b0_ref, b1_ref,
                       a1_ref, a2_ref, a3_ref, cw_ref, wr_ref, wc_ref, wf_ref, wo_ref, xo_ref, u_ref):
    u = cc_ref[...] * cx_ref[...]
    y = b0_ref[...] * cw_ref[0:1, :] + b1_ref[...] * cw_ref[1:2, :] + u * cw_ref[2:3, :]
    y_conv = (cb_ref[...] * y).astype(BF16)
    u_ref[...] = u
    xo_ref[...] = _merge_math(x_ref[...], oret_ref[...].astype(BF16), y_conv, ofox_ref[...].astype(BF16),
                              a1_ref[...], a2_ref[...], a3_ref[...], wr_ref, wc_ref, wf_ref, wo_ref)


def _merge_step(x, z, o_ret, o_fox, buf0, buf1, conv_w, w_ret_o, w_conv_o, w_fox_o, w_o):
    m = x.shape[0]
    zb = lambda j: pl.BlockSpec((m, D_BR), lambda i, j=j: (0, j))
    za = lambda j: pl.BlockSpec((m, D_MODEL), lambda i, j=j: (0, Z_MAIN // D_MODEL + j))
    rows512 = pl.BlockSpec((m, D_BR), lambda i: (0, 0))
    rows = pl.BlockSpec((m, D_MODEL), lambda i: (0, 0))
    return pl.pallas_call(
        _merge_step_kernel,
        grid=(1,),
        in_specs=[rows, rows512, rows512, zb(_CB), zb(_CC), zb(_CX), rows512, rows512,
                  za(0), za(1), za(2), _const_spec(conv_w),
                  _const_spec(w_ret_o), _const_spec(w_conv_o), _const_spec(w_fox_o), _const_spec(w_o)],
        out_specs=[rows, rows512],
        out_shape=[jax.ShapeDtypeStruct((m, D_MODEL), F32),
                   jax.ShapeDtypeStruct((m, D_BR), F32)],
        compiler_params=_cparams(("arbitrary",)),
        name="merge_step",
    )(x, o_ret, o_fox, z, z, z, buf0, buf1, z, z, z, conv_w, w_ret_o, w_conv_o, w_fox_o, w_o)


def _ffn_kernel(x_ref, g_ref, wg_ref, wu_ref, wd_ref, o_ref, h_scr):
    f = pl.program_id(1)

    @pl.when(f == 0)
    def _():
        x = x_ref[...]
        h_scr[...] = _rmsnorm(x, g_ref[...]).astype(BF16)
        o_ref[...] = x

    h = h_scr[...]
    act = (_silu(_dot(h, wg_ref[...])) * _dot(h, wu_ref[...])).astype(BF16)
    o_ref[...] += _dot(act, wd_ref[...])


def _ffn(x, g, wg, wu, wd):
    m = x.shape[0]
    d_ff = wg.shape[1]
    tm = min(m, 512)
    tf = d_ff // 2
    rows = pl.BlockSpec((tm, D_MODEL), lambda i, f: (i, 0))
    return pl.pallas_call(
        _ffn_kernel,
        grid=(m // tm, d_ff // tf),
        in_specs=[rows, pl.BlockSpec((1, D_MODEL), lambda i, f: (0, 0)),
                  pl.BlockSpec((D_MODEL, tf), lambda i, f: (0, f)),
                  pl.BlockSpec((D_MODEL, tf), lambda i, f: (0, f)),
                  pl.BlockSpec((tf, D_MODEL), lambda i, f: (f, 0))],
        out_specs=rows,
        out_shape=jax.ShapeDtypeStruct((m, D_MODEL), F32),
        scratch_shapes=[pltpu.VMEM((tm, D_MODEL), BF16)],
        compiler_params=_cparams(("parallel", "arbitrary")),
        name="ffn",
    )(x, g, wg, wu, wd)


def _router_kernel(x_ref, g_ref, wr_ref, comb_ref):
    h = _rmsnorm(x_ref[...], g_ref[...])
    logits = _dot_f32(h, wr_ref[...])
    lane = lax.broadcasted_iota(jnp.int32, logits.shape, 1)
    valid = lane < N_EXPERTS
    logits = jnp.where(valid, logits, -jnp.inf)
    e = jnp.exp(logits - jnp.max(logits, axis=-1, keepdims=True))
    probs = e / jnp.sum(e, axis=-1, keepdims=True)
    p1 = jnp.max(probs, axis=-1, keepdims=True)
    i1 = jnp.min(jnp.where(probs == p1, lane, LANE), axis=-1, keepdims=True)
    rest = jnp.where(lane == i1, -1.0, probs)
    p2 = jnp.max(rest, axis=-1, keepdims=True)
    i2 = jnp.min(jnp.where(rest == p2, lane, LANE), axis=-1, keepdims=True)
    tot = p1 + p2
    comb_ref[...] = jnp.where(lane == i1, p1 / tot, 0.0) + jnp.where(lane == i2, p2 / tot, 0.0)


def _router(x, g, wr_pad):
    m = x.shape[0]
    tm = min(m, 512)
    return pl.pallas_call(
        _router_kernel,
        grid=(m // tm,),
        in_specs=[pl.BlockSpec((tm, D_MODEL), lambda i: (i, 0)), _const_spec(g), _const_spec(wr_pad)],
        out_specs=pl.BlockSpec((tm, LANE), lambda i: (i, 0)),
        out_shape=jax.ShapeDtypeStruct((m, LANE), F32),
        compiler_params=_cparams(("parallel",)),
        name="router",
    )(x, g, wr_pad)


def _moe_kernel(x_ref, g_ref, comb_ref, wg_ref, wu_ref, wd_ref, o_ref, h_scr):
    e = pl.program_id(1)
    f = pl.program_id(2)

    @pl.when((e == 0) & (f == 0))
    def _():
        x = x_ref[...]
        h_scr[...] = _rmsnorm(x, g_ref[...]).astype(BF16)
        o_ref[...] = x

    h = h_scr[...]
    comb = comb_ref[...]
    lane = lax.broadcasted_iota(jnp.int32, comb.shape, 1)
    ce = jnp.sum(jnp.where(lane == e, comb, 0.0), axis=-1, keepdims=True)
    act = (_silu(_dot(h, wg_ref[0])) * _dot(h, wu_ref[0])).astype(BF16)
    o_ref[...] += ce * _dot(act, wd_ref[0])


def _moe_dense(x, g, comb, wg, wu, wd):
    m = x.shape[0]
    d_exp = wg.shape[2]
    tm = min(m, 512)
    tf = d_exp // 4
    rows = pl.BlockSpec((tm, D_MODEL), lambda i, e, f: (i, 0))
    return pl.pallas_call(
        _moe_kernel,
        grid=(m // tm, N_EXPERTS, d_exp // tf),
        in_specs=[rows, pl.BlockSpec((1, D_MODEL), lambda i, e, f: (0, 0)),
                  pl.BlockSpec((tm, LANE), lambda i, e, f: (i, 0)),
                  pl.BlockSpec((1, D_MODEL, tf), lambda i, e, f: (e, 0, f)),
                  pl.BlockSpec((1, D_MODEL, tf), lambda i, e, f: (e, 0, f)),
                  pl.BlockSpec((1, tf, D_MODEL), lambda i, e, f: (e, f, 0))],
        out_specs=rows,
        out_shape=jax.ShapeDtypeStruct((m, D_MODEL), F32),
        scratch_shapes=[pltpu.VMEM((tm, D_MODEL), BF16)],
        compiler_params=_cparams(("parallel", "arbitrary", "arbitrary")),
        name="moe",
    )(x, g, comb, wg, wu, wd)


def _final_norm_kernel(x_ref, g_ref, o_ref):
    o_ref[...] = _rmsnorm(x_ref[...], g_ref[...])


def _final_norm(x, g):
    m = x.shape[0]
    tm = min(m, 1024)
    rows = pl.BlockSpec((tm, D_MODEL), lambda i: (i, 0))
    return pl.pallas_call(
        _final_norm_kernel,
        grid=(m // tm,),
        in_specs=[rows, _const_spec(g)],
        out_specs=rows,
        out_shape=jax.ShapeDtypeStruct((m, D_MODEL), F32),
        compiler_params=_cparams(("parallel",)),
        name="final_norm",
    )(x, g)


def _forget_step_kernel(ff_ref, b_ref, lf_ref):
    lf_ref[...] = _log_sigmoid(ff_ref[...] + b_ref[...])


def _forget_step(z, b_pad):
    m = z.shape[0]
    return pl.pallas_call(
        _forget_step_kernel,
        grid=(1,),
        in_specs=[pl.BlockSpec((m, LANE), lambda i: (0, Z_FF // LANE)), _const_spec(b_pad)],
        out_specs=pl.BlockSpec((m, LANE), lambda i: (0, 0)),
        out_shape=jax.ShapeDtypeStruct((m, LANE), F32),
        name="forget_step",
    )(z, b_pad)


def _prep_w_in(w):
    main = w[:, :Z_MAIN]
    ff = w[:, Z_MAIN:Z_MAIN + H]
    gates = w[:, Z_MAIN + H:]
    ff = jnp.pad(ff, ((0, 0), (0, LANE - H)))
    return jnp.concatenate([main, gates, ff], axis=1).astype(BF16)


def _rope_tables(pos):
    half = HD // 2
    inv = ROPE_BASE ** (-jnp.arange(half, dtype=F32) / half)
    ang = pos.astype(F32)[:, None] * inv[None, :]
    cos = jnp.cos(ang)
    sin = jnp.sin(ang)
    cos64 = jnp.concatenate([cos, cos], axis=1)
    sin64 = jnp.concatenate([-sin, sin], axis=1)
    return cos64, sin64


def kernel(x_prompt, x_sample, state_ret, state_conv, cache_k, cache_v, cache_logf, page_table, g_mix, w_in,
           b_forget, conv_w, w_ret_o, w_conv_o, w_fox_o, w_o, g_ffn, w_ffn_gate, w_ffn_up, w_ffn_down, w_router,
           w_exp_gate, w_exp_up, w_exp_down, g_final):
    b, l, _ = x_prompt.shape
    bd, t, _ = x_sample.shape
    assert t == 1, "decode group carries one new position per sequence"
    depth = w_in.shape[0]
    n_pages = page_table.shape[1]
    past_len = n_pages * PAGE
    n_phys = cache_k.shape[1]

    log_gamma = jnp.log(1.0 - jnp.exp2(-5.0 - jnp.arange(H, dtype=F32)))
    tables = _ret_tables(log_gamma)
    gamma = jnp.exp(log_gamma)
    cos_p, sin_p = _rope_tables(jnp.arange(l))
    cos_p = jnp.tile(cos_p, (1, H))
    sin_p = jnp.tile(sin_p, (1, H))
    cos_s, sin_s = _rope_tables(past_len + jnp.arange(1))
    cos_s = jnp.tile(cos_s, (1, H))
    sin_s = jnp.tile(sin_s, (1, H))

    ck = cache_k.reshape(depth, n_phys, PAGE, D_BR)
    cv = cache_v.reshape(depth, n_phys, PAGE, D_BR)
    clf_t = jnp.swapaxes(cache_logf, 2, 3)

    xp = x_prompt.reshape(b * l, D_MODEL)
    xs = x_sample.reshape(bd, D_MODEL)
    outs = {k: [] for k in ("ret_p", "ret_s", "buf_p", "buf_s", "kp", "vp", "lfp", "ks", "vs", "lfs")}
    for layer in range(depth):
        w_in_l = _prep_w_in(w_in[layer])
        g_l = g_mix[layer].reshape(1, D_MODEL)
        b_pad = jnp.pad(b_forget[layer], (0, LANE - H)).reshape(1, LANE)
        wr_o = w_ret_o[layer].astype(BF16)
        wc_o = w_conv_o[layer].astype(BF16)
        wf_o = w_fox_o[layer].astype(BF16)
        wo = w_o[layer].astype(BF16)
        cw = conv_w[layer]

        z = _inproj(xp, g_l, w_in_l)
        o_ret, st = _ret_prompt(z, cos_p, sin_p, tables, b, l)
        lf, ccol, crow = _forget_prompt(z, b_pad, b, l)
        o_fox = _fox_prompt(z, ccol, crow, b, l)
        xp, buf = _merge_prompt(xp, z, o_ret, o_fox, cw, wr_o, wc_o, wf_o, wo, b, l)
        outs["ret_p"].append(st)
        outs["buf_p"].append(buf)
        outs["kp"].append(z[:, _FK * D_BR:(_FK + 1) * D_BR].reshape(b, l, H, HD))
        outs["vp"].append(z[:, _FV * D_BR:(_FV + 1) * D_BR].reshape(b, l, H, HD))
        outs["lfp"].append(lf.reshape(b, l, H))

        zs = _inproj(xs, g_l, w_in_l)
        seg = lambda j: zs[:, j * D_BR:(j + 1) * D_BR]
        o_ret_s, st_s = _ret_step(zs, state_ret[layer], cos_s, sin_s, gamma)
        lf_s = _forget_step(zs, b_pad)[:, :H]
        o_fox_s = _fox_decode(page_table, seg(_FQ), seg(_FK), seg(_FV), lf_s, ck, cv, clf_t, layer)
        xs, u_s = _merge_step(xs, zs, o_ret_s, o_fox_s, state_conv[layer, :, 0], state_conv[layer, :, 1],
                              cw, wr_o, wc_o, wf_o, wo)
        outs["ret_s"].append(st_s)
        outs["buf_s"].append(jnp.stack([state_conv[layer, :, 1], u_s], axis=1))
        outs["ks"].append(seg(_FK).reshape(bd, 1, H, HD))
        outs["vs"].append(seg(_FV).reshape(bd, 1, H, HD))
        outs["lfs"].append(lf_s.reshape(bd, 1, H))

        g_f = g_ffn[layer].reshape(1, D_MODEL)
        mi = layer // 2
        if layer % 2 == 0:
            wg = w_ffn_gate[mi].astype(BF16)
            wu = w_ffn_up[mi].astype(BF16)
            wd = w_ffn_down[mi].astype(BF16)
            xp = _ffn(xp, g_f, wg, wu, wd)
            xs = _ffn(xs, g_f, wg, wu, wd)
        else:
            wr_pad = jnp.pad(w_router[mi], ((0, 0), (0, LANE - N_EXPERTS)))
            wg = w_exp_gate[mi].astype(BF16)
            wu = w_exp_up[mi].astype(BF16)
            wd = w_exp_down[mi].astype(BF16)
            xp = _moe_dense(xp, g_f, _router(xp, g_f, wr_pad), wg, wu, wd)
            xs = _moe_dense(xs, g_f, _router(xs, g_f, wr_pad), wg, wu, wd)

    g_fin = g_final.reshape(1, D_MODEL)
    y_prompt = _final_norm(xp, g_fin).reshape(b, l, D_MODEL)
    y_sample = _final_norm(xs, g_fin).reshape(bd, 1, D_MODEL)
    st = lambda k: jnp.stack(outs[k])
    return (y_prompt, y_sample, st("ret_p"), st("ret_s"), st("buf_p"), st("buf_s"),
            st("kp"), st("vp"), st("lfp"), st("ks"), st("vs"), st("lfs"))
```

```python
import functools

import jax
import jax.numpy as jnp
import numpy as np
from jax import lax
from jax.experimental import pallas as pl
from jax.experimental.pallas import tpu as pltpu

F32 = jnp.float32
BF16 = jnp.bfloat16

D_MODEL = 1024
H = 8
HD = 64
D_BR = H * HD
RET_CHUNK = 128
PAGE = 128
FLAT = PAGE * H
N_EXPERTS = 8
ROPE_BASE = 10000.0
EPS = 1e-6
LOG2E = 1.4426950408889634
LANE = 128
VMEM_LIMIT = 56 * 1024 * 1024

_RQ, _RK, _RV, _RG, _CB, _CC, _CX, _FQ, _FK, _FV = range(10)
Z_MAIN = 10 * D_BR
Z_COLS = Z_MAIN + 3 * D_MODEL
Z_TN = 2048
FOX_TQ = 1024
FOX_TK = 1024
FOX_TD = 512
RET_ROWS = 512
MOE_TM = 512


def _cparams(sem):
    return pltpu.CompilerParams(dimension_semantics=sem, vmem_limit_bytes=VMEM_LIMIT)


def _rmsnorm(x, g):
    return x * lax.rsqrt(jnp.mean(x * x, axis=-1, keepdims=True) + EPS) * g


def _dot(a, b):
    return jnp.dot(a, b, preferred_element_type=F32)


def _dot_nt(a, b):
    return lax.dot_general(a, b, (((1,), (1,)), ((), ())), preferred_element_type=F32)


def _dot_tn(a, b):
    return lax.dot_general(a, b, (((0,), (0,)), ((), ())), preferred_element_type=F32)


def _dot_f32(a, b):
    return jnp.dot(a, b, preferred_element_type=F32, precision=lax.Precision.HIGHEST)


def _dot_nt_f32(a, b):
    return lax.dot_general(a, b, (((1,), (1,)), ((), ())), preferred_element_type=F32,
                           precision=lax.Precision.HIGHEST)


def _sigmoid(x):
    return 1.0 / (1.0 + jnp.exp(-x))


def _silu(x):
    return x * _sigmoid(x)


def _log_sigmoid(x):
    return jnp.minimum(x, 0.0) - jnp.log(1.0 + jnp.exp(-jnp.abs(x)))


def _const_spec(a):
    n = a.ndim
    return pl.BlockSpec(a.shape, lambda *_, n=n: (0,) * n)


def _inproj_kernel(x_ref, g_ref, w_ref, wf_ref, z_ref, zb_ref, ff_ref, h_scr):
    j = pl.program_id(1)

    @pl.when(j == 0)
    def _():
        h = _rmsnorm(x_ref[...], g_ref[...]).astype(BF16)
        h_scr[...] = h
        ff_ref[...] = _dot(h, wf_ref[...])

    z = _dot(h_scr[...], w_ref[...])
    z_ref[...] = z

    @pl.when(j == 1)
    def _():
        zb_ref[:, 0:D_BR] = (z[:, 3 * D_BR:] * (HD ** -0.5 * LOG2E)).astype(BF16)

    @pl.when(j == 2)
    def _():
        zb_ref[:, D_BR:] = z[:, :2 * D_BR].astype(BF16)


def _inproj(x, g, w, w_ff):
    m = x.shape[0]
    tm = min(m, 1024)
    assert Z_COLS // Z_TN == 4 and _FQ * D_BR == Z_TN + 3 * D_BR and _FK * D_BR == 2 * Z_TN
    return pl.pallas_call(
        _inproj_kernel,
        grid=(m // tm, Z_COLS // Z_TN),
        in_specs=[
            pl.BlockSpec((tm, D_MODEL), lambda i, j: (i, 0)),
            pl.BlockSpec((1, D_MODEL), lambda i, j: (0, 0)),
            pl.BlockSpec((D_MODEL, Z_TN), lambda i, j: (0, j)),
            pl.BlockSpec((D_MODEL, LANE), lambda i, j: (0, 0)),
        ],
        out_specs=[pl.BlockSpec((tm, Z_TN), lambda i, j: (i, j)),
                   pl.BlockSpec((tm, 3 * D_BR), lambda i, j: (i, 0)),
                   pl.BlockSpec((tm, LANE), lambda i, j: (i, 0))],
        out_shape=[jax.ShapeDtypeStruct((m, Z_COLS), F32),
                   jax.ShapeDtypeStruct((m, 3 * D_BR), BF16),
                   jax.ShapeDtypeStruct((m, LANE), F32)],
        scratch_shapes=[pltpu.VMEM((tm, D_MODEL), BF16)],
        compiler_params=_cparams(("parallel", "arbitrary")),
        name="inproj",
    )(x, g, w, w_ff)


def _rope(x, cos, sin_signed):
    lane = lax.broadcasted_iota(jnp.int32, x.shape, 1)
    first = (lane % HD) < (HD // 2)
    n = x.shape[1]
    partner = jnp.where(first, pltpu.roll(x, n - HD // 2, 1), pltpu.roll(x, HD // 2, 1))
    return x * cos + partner * sin_signed


def _ret_prompt_kernel(rq_ref, rk_ref, rv_ref, rg_ref, cos_ref, sin_ref, dec_ref, qd_ref, kd_ref, sd_ref,
                       o_ref, st_ref, s_scr, *, chunks):
    c = pl.program_id(1)

    @pl.when(c == 0)
    def _():
        s_scr[...] = jnp.zeros_like(s_scr)

    low = lax.broadcasted_iota(jnp.int32, (RET_CHUNK, LANE), 1) < HD
    row = lax.broadcasted_iota(jnp.int32, (LANE, LANE), 0)
    col = lax.broadcasted_iota(jnp.int32, (LANE, LANE), 1)
    same_head = (row < HD) == (col < HD)
    states = [s_scr[p] for p in range(H // 2)]
    for ch in range(chunks):
        rs = slice(ch * RET_CHUNK, (ch + 1) * RET_CHUNK)
        cos = cos_ref[rs, :]
        sin = sin_ref[rs, :]
        q = _rope(rq_ref[rs, :], cos, sin)
        k = _rope(rk_ref[rs, :], cos, sin) * (HD ** -0.5)
        v = rv_ref[rs, :]
        for p in range(H // 2):
            sl = slice(p * LANE, (p + 1) * LANE)
            qp = q[:, sl]
            kp = k[:, sl].astype(BF16)
            vp = v[:, sl].astype(BF16)
            inner = []
            for hh in range(2):
                qm = jnp.where(low if hh == 0 else jnp.logical_not(low), qp, 0.0).astype(BF16)
                sc = _dot_nt(qm, kp) * dec_ref[2 * p + hh]
                inner.append(_dot(sc.astype(BF16), vp))
            s_old = states[p]
            cross = _dot(qp.astype(BF16), s_old.astype(BF16)) * qd_ref[:, sl]
            o = jnp.where(low, inner[0], inner[1]) + cross
            kdec = (k[:, sl] * kd_ref[:, sl]).astype(BF16)
            states[p] = sd_ref[p] * s_old + jnp.where(same_head, _dot_tn(kdec, vp), 0.0)
            oo = o * o
            ms = jnp.where(low, jnp.sum(jnp.where(low, oo, 0.0), axis=-1, keepdims=True),
                           jnp.sum(jnp.where(low, 0.0, oo), axis=-1, keepdims=True)) * (1.0 / HD)
            o_ref[rs, sl] = (o * lax.rsqrt(ms + EPS) * _silu(rg_ref[rs, sl])).astype(BF16)
    for p in range(H // 2):
        s_scr[p] = states[p]

    @pl.when(c == pl.num_programs(1) - 1)
    def _():
        for p in range(H // 2):
            st_ref[0, 2 * p] = states[p][:HD, :HD]
            st_ref[0, 2 * p + 1] = states[p][HD:, HD:]


def _ret_tables(log_gamma):
    i = jnp.arange(RET_CHUNK, dtype=F32)
    diff = i[:, None] - i[None, :]
    decay = jnp.where(diff >= 0, jnp.exp(log_gamma[:, None, None] * jnp.maximum(diff, 0.0)), 0.0)
    q_decay = jnp.exp(log_gamma[:, None] * (i + 1.0)[None, :]).T
    k_decay = jnp.exp(log_gamma[:, None] * (RET_CHUNK - 1.0 - i)[None, :]).T
    s_decay = jnp.exp(log_gamma * RET_CHUNK)
    qd = jnp.repeat(q_decay, HD, axis=1)
    kd = jnp.repeat(k_decay, HD, axis=1)
    sd = jnp.broadcast_to(jnp.repeat(s_decay, HD).reshape(H // 2, LANE, 1), (H // 2, LANE, LANE))
    return decay, qd, kd, sd


def _ret_prompt(z, cos, sin, tables, b, l):
    decay, qd, kd, sd = tables
    tr = min(l, RET_ROWS)
    nc = l // tr
    blk = lambda j: pl.BlockSpec((tr, D_BR), lambda bi, c, j=j: (bi * nc + c, j))
    tab = pl.BlockSpec((tr, D_BR), lambda bi, c: (c, 0))
    return pl.pallas_call(
        functools.partial(_ret_prompt_kernel, chunks=tr // RET_CHUNK),
        grid=(b, nc),
        in_specs=[blk(_RQ), blk(_RK), blk(_RV), blk(_RG), tab, tab,
                  _const_spec(decay), _const_spec(qd), _const_spec(kd), _const_spec(sd)],
        out_specs=[pl.BlockSpec((tr, D_BR), lambda bi, c: (bi * nc + c, 0)),
                   pl.BlockSpec((1, H, HD, HD), lambda bi, c: (bi, 0, 0, 0))],
        out_shape=[jax.ShapeDtypeStruct((b * l, D_BR), BF16),
                   jax.ShapeDtypeStruct((b, H, HD, HD), F32)],
        scratch_shapes=[pltpu.VMEM((H // 2, LANE, LANE), F32)],
        compiler_params=_cparams(("parallel", "arbitrary")),
        name="ret_prompt",
    )(z, z, z, z, cos, sin, decay, qd, kd, sd)


def _ret_step_kernel(q_ref, k_ref, v_ref, g_ref, s_ref, cos_ref, sin_ref, gam_ref, grow_ref, o_ref, sn_ref):
    cos = cos_ref[...]
    sin = sin_ref[...]
    q = _rope(q_ref[0], cos, sin).astype(BF16)
    k = (_rope(k_ref[0], cos, sin) * (HD ** -0.5)).astype(BF16)
    own = (lax.broadcasted_iota(jnp.int32, (H, D_BR), 1) // HD) == lax.broadcasted_iota(jnp.int32, (H, D_BR), 0)
    q_bd = jnp.where(own, jnp.broadcast_to(q.astype(F32), (H, D_BR)), 0.0)
    k_bd = jnp.where(own, jnp.broadcast_to(k.astype(F32), (H, D_BR)), 0.0)
    v = v_ref[0].astype(BF16)
    sc = jnp.sum(q_bd * k.astype(F32), axis=-1, keepdims=True)
    inner = sc.astype(BF16).astype(F32) * v.astype(F32)
    s_old = s_ref[0]
    cross = _dot(q_bd.astype(BF16), s_old.astype(BF16)) * gam_ref[...]
    o = inner + cross
    o = o * lax.rsqrt(jnp.mean(o * o, axis=-1, keepdims=True) + EPS)
    o_ref[0] = o * _silu(g_ref[0])
    sn_ref[0] = grow_ref[...] * s_old + _dot_tn(k_bd.astype(BF16), v)


def _ret_step(zs, state, cos512, sin512, gamma):
    bd = zs.shape[0]
    seg = lambda j: zs[:, j * D_BR:(j + 1) * D_BR]
    s2 = state.reshape(bd, D_BR, HD)
    gam = gamma.reshape(H, 1)
    grow = jnp.broadcast_to(jnp.repeat(gamma, HD)[:, None], (D_BR, HD))
    flat = pl.BlockSpec((1, 1, D_BR), lambda i: (i, 0, 0))
    r3 = pl.BlockSpec((1, H, HD), lambda i: (i, 0, 0))
    s3 = pl.BlockSpec((1, D_BR, HD), lambda i: (i, 0, 0))
    o, sn = pl.pallas_call(
        _ret_step_kernel,
        grid=(bd,),
        in_specs=[flat, flat, r3, r3, s3,
                  _const_spec(cos512), _const_spec(sin512), _const_spec(gam), _const_spec(grow)],
        out_specs=[r3, s3],
        out_shape=[jax.ShapeDtypeStruct((bd, H, HD), F32),
                   jax.ShapeDtypeStruct((bd, D_BR, HD), F32)],
        compiler_params=_cparams(("parallel",)),
        name="ret_step",
    )(seg(_RQ).reshape(bd, 1, D_BR), seg(_RK).reshape(bd, 1, D_BR), seg(_RV).reshape(bd, H, HD),
      seg(_RG).reshape(bd, H, HD), s2, cos512, sin512, gam, grow)
    return o.reshape(bd, D_BR), sn.reshape(bd, H, HD, HD)


def _forget_prompt_kernel(ff_ref, b_ref, tri_ref, lf_ref, bias_ref, carry):
    @pl.when(pl.program_id(1) == 0)
    def _():
        carry[...] = jnp.zeros_like(carry)

    lf = _log_sigmoid(ff_ref[...] + b_ref[...])
    lf_ref[...] = lf[:, :H]
    c = _dot_f32(tri_ref[...], lf) + carry[...]
    bias_ref[0] = c.T[:H, :] * (-LOG2E)
    carry[...] = c[-1:, :]


def _forget_prompt(ff, b_pad, b, l):
    tc = min(l, 512)
    nt = l // tc
    tri = jnp.asarray(np.tril(np.ones((tc, tc), np.float32)))
    return pl.pallas_call(
        _forget_prompt_kernel,
        grid=(b, nt),
        in_specs=[pl.BlockSpec((tc, LANE), lambda bi, t: (bi * nt + t, 0)),
                  _const_spec(b_pad), _const_spec(tri)],
        out_specs=[pl.BlockSpec((tc, H), lambda bi, t: (bi * nt + t, 0)),
                   pl.BlockSpec((1, H, tc), lambda bi, t: (bi, 0, t))],
        out_shape=[jax.ShapeDtypeStruct((b * l, H), F32),
                   jax.ShapeDtypeStruct((b, H, l), F32)],
        scratch_shapes=[pltpu.VMEM((1, LANE), F32)],
        compiler_params=_cparams(("parallel", "arbitrary")),
        name="forget_prompt",
    )(ff, b_pad, tri)


def _forget_step_kernel(ff_ref, b_ref, lf_ref):
    lf_ref[...] = _log_sigmoid(ff_ref[...] + b_ref[...])


def _forget_step(ff, b_pad):
    m = ff.shape[0]
    full = pl.BlockSpec((m, LANE), lambda i: (0, 0))
    return pl.pallas_call(
        _forget_step_kernel,
        grid=(1,),
        in_specs=[full, _const_spec(b_pad)],
        out_specs=full,
        out_shape=jax.ShapeDtypeStruct((m, LANE), F32),
        name="forget_step",
    )(ff, b_pad)


def _fox_prompt_kernel(q_ref, k_ref, v_ref, bias_ref, o_ref, *, tq, tk, td):
    qi = pl.program_id(2)
    zero = jnp.zeros((), BF16)
    one = jnp.ones((), BF16)
    lowq = lax.broadcasted_iota(jnp.int32, (tq, LANE), 1) < HD
    q = q_ref[...]
    qm = (jnp.where(lowq, q, zero), jnp.where(lowq, zero, q))

    def step(ks, width, r0, carry, masked):
        rows = tq - r0
        lowk = lax.broadcasted_iota(jnp.int32, (width, LANE), 1) < HD
        kb = k_ref[pl.ds(ks, width), :]
        vb = v_ref[pl.ds(ks, width), :]
        vh = (jnp.where(lowk, vb, one), jnp.where(lowk, one, vb))
        new = []
        for hh in range(2):
            m_all, acc_all = carry[hh]
            m, acc = m_all[r0:], acc_all[r0:]
            s = _dot_nt(qm[hh][r0:], kb) + bias_ref[0, 0, pl.ds(hh, 1), pl.ds(ks, width)]
            if masked:
                qpos = qi * tq + r0 + lax.broadcasted_iota(jnp.int32, (rows, width), 0)
                kpos = ks + lax.broadcasted_iota(jnp.int32, (rows, width), 1)
                s = jnp.where(kpos <= qpos, s, -jnp.inf)
            m_new = jnp.maximum(m, jnp.max(s, axis=-1, keepdims=True))
            p = jnp.exp2(s - m_new)
            acc = jnp.exp2(m - m_new) * acc + _dot(p.astype(BF16), vh[hh])
            if r0:
                m_new = jnp.concatenate([m_all[:r0], m_new], axis=0)
                acc = jnp.concatenate([acc_all[:r0], acc], axis=0)
            new.append((m_new, acc))
        return tuple(new)

    init = tuple((jnp.full((tq, 1), -jnp.inf, F32), jnp.zeros((tq, LANE), F32)) for _ in range(2))
    n_full = qi * (tq // tk)
    carry = lax.fori_loop(0, n_full, lambda j, c: step(pl.multiple_of(j * tk, tk), tk, 0, c, False), init)
    for d in range(tq // td):
        carry = step(pl.multiple_of(qi * tq + d * td, td), td, d * td, carry, True)
    (_, acc0), (_, acc1) = carry
    out0 = acc0 / acc0[:, HD:HD + 1]
    out1 = acc1 / acc1[:, 0:1]
    o_ref[...] = jnp.where(lowq, out0, out1).astype(BF16)


def _fox_prompt(zb, bias, b, l):
    tq = min(l, FOX_TQ)
    tk = min(tq, FOX_TK)
    td = min(tk, FOX_TD)
    nq = l // tq
    npair = H // 2
    qspec = pl.BlockSpec((tq, LANE), lambda bi, p, qi: (bi * nq + qi, p))
    kspec = pl.BlockSpec((l, LANE), lambda bi, p, qi: (bi, npair + p))
    vspec = pl.BlockSpec((l, LANE), lambda bi, p, qi: (bi, 2 * npair + p))
    return pl.pallas_call(
        functools.partial(_fox_prompt_kernel, tq=tq, tk=tk, td=td),
        grid=(b, npair, nq),
        in_specs=[qspec, kspec, vspec,
                  pl.BlockSpec((1, 1, 2, l), lambda bi, p, qi: (bi, p, 0, 0))],
        out_specs=qspec,
        out_shape=jax.ShapeDtypeStruct((b * l, D_BR), BF16),
        compiler_params=_cparams(("parallel", "parallel", "arbitrary")),
        name="fox_prompt",
    )(zb, zb, zb, bias)


def _per_head_all(x, op):
    s = H
    while s < FLAT:
        x = op(x, pltpu.roll(x, s, 1))
        s *= 2
    return x


def _fox_decode_kernel(pt_ref, q_ref, kn_ref, vn_ref, lfn_ref, *rest, g):
    del pt_ref
    k_refs = rest[:g]
    v_refs = rest[g:2 * g]
    lf_refs = rest[2 * g:3 * g]
    o_ref = rest[3 * g]
    m_scr, l_scr, acc_scr, cy_scr = rest[3 * g + 1:]
    j = pl.program_id(1)
    lane8 = lax.broadcasted_iota(jnp.int32, (H, FLAT), 1)
    row8 = lax.broadcasted_iota(jnp.int32, (H, FLAT), 0)
    diag = (lane8 % H) == row8
    qb = (q_ref[0] * (HD ** -0.5 * LOG2E)).astype(BF16)

    def scores(kr):
        return jnp.sum(jnp.where(diag, _dot_nt(qb, kr), 0.0), axis=0, keepdims=True)

    def column(flat):
        return jnp.sum(jnp.where(lane8 == row8, jnp.broadcast_to(flat, (H, FLAT)), 0.0), axis=1, keepdims=True)

    @pl.when(j == 0)
    def _():
        kn = jnp.broadcast_to(kn_ref[0].astype(BF16)[None], (PAGE, H, HD)).reshape(FLAT, HD)
        m_scr[...] = scores(kn)
        l_scr[...] = jnp.ones_like(l_scr)
        acc_scr[...] = vn_ref[0].astype(BF16).astype(F32)
        cy_scr[...] = lfn_ref[0] * LOG2E

    lf = jnp.concatenate([lf_refs[t][0, 0] for t in range(g)], axis=0) * LOG2E
    lane = lax.broadcasted_iota(jnp.int32, (g, FLAT), 1)
    suf = lf
    s = H
    while s < FLAT:
        suf = suf + jnp.where(lane < FLAT - s, pltpu.roll(suf, FLAT - s, 1), 0.0)
        s *= 2
    tot = _per_head_all(jnp.where(lane < H, suf, 0.0), jnp.add)
    cy = cy_scr[...]
    rows = []
    for t in range(g):
        kr = k_refs[t][0, 0].reshape(FLAT, HD).astype(BF16)
        rows.append(scores(kr) + cy)
        cy = cy + tot[t:t + 1]
    cy_scr[...] = cy
    logits = jnp.concatenate(rows, axis=0) + (suf - lf)
    m_old = m_scr[...]
    m_new = jnp.maximum(m_old, _per_head_all(jnp.max(logits, axis=0, keepdims=True), jnp.maximum))
    alpha = jnp.exp2(m_old - m_new)
    p = jnp.exp2(logits - m_new)
    l_scr[...] = alpha * l_scr[...] + _per_head_all(jnp.sum(p, axis=0, keepdims=True), jnp.add)
    m_scr[...] = m_new
    pv = jnp.zeros((H, HD), F32)
    for t in range(g):
        vr = v_refs[t][0, 0].reshape(FLAT, HD).astype(BF16)
        pm = jnp.where(diag, jnp.broadcast_to(p[t:t + 1], (H, FLAT)), 0.0).astype(BF16)
        pv = pv + _dot(pm, vr)
    acc_scr[...] = column(alpha) * acc_scr[...] + pv

    @pl.when(j == pl.num_programs(1) - 1)
    def _():
        o_ref[0] = acc_scr[...] / column(l_scr[...])


def _fox_decode(page_table, q8, kn8, vn8, lfn_flat, cache_k, cache_v, cache_lf_flat, layer):
    bd, n_pages = page_table.shape
    g = 8 if n_pages % 8 == 0 else 1
    steps = n_pages // g

    def page(nd):
        def spec(t):
            return lambda b, j, pt, t=t: (layer, pt[b, n_pages - 1 - (j * g + t)]) + (0,) * nd
        return spec

    r3 = pl.BlockSpec((1, H, HD), lambda b, j, pt: (b, 0, 0))
    in_specs = [r3, r3, r3, pl.BlockSpec((1, 1, FLAT), lambda b, j, pt: (b, 0, 0))]
    in_specs += [pl.BlockSpec((1, 1, PAGE, H, HD), page(3)(t)) for t in range(g)]
    in_specs += [pl.BlockSpec((1, 1, PAGE, H, HD), page(3)(t)) for t in range(g)]
    in_specs += [pl.BlockSpec((1, 1, 1, FLAT), page(2)(t)) for t in range(g)]
    grid_spec = pltpu.PrefetchScalarGridSpec(
        num_scalar_prefetch=1,
        grid=(bd, steps),
        in_specs=in_specs,
        out_specs=r3,
        scratch_shapes=[pltpu.VMEM((1, FLAT), F32), pltpu.VMEM((1, FLAT), F32),
                        pltpu.VMEM((H, HD), F32), pltpu.VMEM((1, FLAT), F32)],
    )
    return pl.pallas_call(
        functools.partial(_fox_decode_kernel, g=g),
        grid_spec=grid_spec,
        out_shape=jax.ShapeDtypeStruct((bd, H, HD), F32),
        compiler_params=_cparams(("parallel", "arbitrary")),
        name="fox_decode",
    )(page_table, q8, kn8, vn8, lfn_flat, *([cache_k] * g), *([cache_v] * g), *([cache_lf_flat] * g))


def _merge_math(x, o_ret, y_conv, o_fox, a1, a2, a3, wr_ref, wc_ref, wf_ref, wo_ref):
    merged = (_sigmoid(a1) * _dot(o_ret, wr_ref[...])
              + _sigmoid(a2) * _dot(y_conv, wc_ref[...])
              + _sigmoid(a3) * _dot(o_fox, wf_ref[...]))
    return x + _dot(merged.astype(BF16), wo_ref[...])


def _merge_prompt_kernel(x_ref, oret_ref, ofox_ref, cb_ref, cc_ref, cx_ref, ccp_ref, cxp_ref,
                         a1_ref, a2_ref, a3_ref, cw_ref, wr_ref, wc_ref, wf_ref, wo_ref,
                         xo_ref, buf_ref, *, tm, tiles_per_seq):
    i = pl.program_id(0)
    u = cc_ref[...] * cx_ref[...]
    prev = ccp_ref[...] * cxp_ref[...]
    prev = jnp.where(i % tiles_per_seq == 0, 0.0, prev)
    p1 = prev[7:8, :]
    p2 = prev[6:7, :]
    row = lax.broadcasted_iota(jnp.int32, u.shape, 0)
    u1 = jnp.where(row >= 1, pltpu.roll(u, 1, 0), p1)
    u2 = jnp.where(row >= 2, pltpu.roll(u, 2, 0), jnp.where(row == 1, p1, p2))
    y = u2 * cw_ref[0:1, :] + u1 * cw_ref[1:2, :] + u * cw_ref[2:3, :]
    y_conv = (cb_ref[...] * y).astype(BF16)
    buf_ref[0] = u[tm - 2:, :]
    xo_ref[...] = _merge_math(x_ref[...], oret_ref[...], y_conv, ofox_ref[...],
                              a1_ref[...], a2_ref[...], a3_ref[...], wr_ref, wc_ref, wf_ref, wo_ref)


def _merge_prompt(x, z, o_ret, o_fox, conv_w, w_ret_o, w_conv_o, w_fox_o, w_o, b, l):
    m = b * l
    tm = min(l, 512)
    tps = l // tm
    zb = lambda j: pl.BlockSpec((tm, D_BR), lambda i, j=j: (i, j))
    zprev = lambda j: pl.BlockSpec((8, D_BR), lambda i, j=j: (jnp.maximum(i * (tm // 8) - 1, 0), j))
    za = lambda j: pl.BlockSpec((tm, D_MODEL), lambda i, j=j: (i, Z_MAIN // D_MODEL + j))
    rows512 = pl.BlockSpec((tm, D_BR), lambda i: (i, 0))
    rows = pl.BlockSpec((tm, D_MODEL), lambda i: (i, 0))
    return pl.pallas_call(
        functools.partial(_merge_prompt_kernel, tm=tm, tiles_per_seq=tps),
        grid=(m // tm,),
        in_specs=[rows, rows512, rows512, zb(_CB), zb(_CC), zb(_CX), zprev(_CC), zprev(_CX),
                  za(0), za(1), za(2), _const_spec(conv_w),
                  _const_spec(w_ret_o), _const_spec(w_conv_o), _const_spec(w_fox_o), _const_spec(w_o)],
        out_specs=[rows, pl.BlockSpec((1, 2, D_BR), lambda i: (i // tps, 0, 0))],
        out_shape=[jax.ShapeDtypeStruct((m, D_MODEL), F32),
                   jax.ShapeDtypeStruct((b, 2, D_BR), F32)],
        compiler_params=_cparams(("arbitrary",)),
        name="merge_prompt",
    )(x, o_ret, o_fox, z, z, z, z, z, z, z, z, conv_w, w_ret_o, w_conv_o, w_fox_o, w_o)


def _merge_step_kernel(x_ref, oret_ref, ofox_ref, cb_ref, cc_ref, cx_ref, b0_ref, b1_ref,
                       a1_ref, a2_ref, a3_ref, cw_ref, wr_ref, wc_ref, wf_ref, wo_ref, xo_ref, u_ref):
    u = cc_ref[...] * cx_ref[...]
    y = b0_ref[...] * cw_ref[0:1, :] + b1_ref[...] * cw_ref[1:2, :] + u * cw_ref[2:3, :]
    y_conv = (cb_ref[...] * y).astype(BF16)
    u_ref[...] = u
    xo_ref[...] = _merge_math(x_ref[...], oret_ref[...].astype(BF16), y_conv, ofox_ref[...].astype(BF16),
                              a1_ref[...], a2_ref[...], a3_ref[...], wr_ref, wc_ref, wf_ref, wo_ref)


def _merge_step(x, z, o_ret, o_fox, buf0, buf1, conv_w, w_ret_o, w_conv_o, w_fox_o, w_o):
    m = x.shape[0]
    zb = lambda j: pl.BlockSpec((m, D_BR), lambda i, j=j: (0, j))
    za = lambda j: pl.BlockSpec((m, D_MODEL), lambda i, j=j: (0, Z_MAIN // D_MODEL + j))
    rows512 = pl.BlockSpec((m, D_BR), lambda i: (0, 0))
    rows = pl.BlockSpec((m, D_MODEL), lambda i: (0, 0))
    return pl.pallas_call(
        _merge_step_kernel,
        grid=(1,),
        in_specs=[rows, rows512, rows512, zb(_CB), zb(_CC), zb(_CX), rows512, rows512,
                  za(0), za(1), za(2), _const_spec(conv_w),
                  _const_spec(w_ret_o), _const_spec(w_conv_o), _const_spec(w_fox_o), _const_spec(w_o)],
        out_specs=[rows, rows512],
        out_shape=[jax.ShapeDtypeStruct((m, D_MODEL), F32),
                   jax.ShapeDtypeStruct((m, D_BR), F32)],
        compiler_params=_cparams(("arbitrary",)),
        name="merge_step",
    )(x, o_ret, o_fox, z, z, z, buf0, buf1, z, z, z, conv_w, w_ret_o, w_conv_o, w_fox_o, w_o)


def _ffn_kernel(x_ref, g_ref, wg_ref, wu_ref, wd_ref, o_ref, h_scr):
    f = pl.program_id(1)

    @pl.when(f == 0)
    def _():
        x = x_ref[...]
        h_scr[...] = _rmsnorm(x, g_ref[...]).astype(BF16)
        o_ref[...] = x

    h = h_scr[...]
    act = (_silu(_dot(h, wg_ref[...])) * _dot(h, wu_ref[...])).astype(BF16)
    o_ref[...] += _dot(act, wd_ref[...])


def _ffn(x, g, wg, wu, wd):
    m = x.shape[0]
    d_ff = wg.shape[1]
    tm = min(m, 512)
    tf = d_ff // 2
    rows = pl.BlockSpec((tm, D_MODEL), lambda i, f: (i, 0))
    return pl.pallas_call(
        _ffn_kernel,
        grid=(m // tm, d_ff // tf),
        in_specs=[rows, pl.BlockSpec((1, D_MODEL), lambda i, f: (0, 0)),
                  pl.BlockSpec((D_MODEL, tf), lambda i, f: (0, f)),
                  pl.BlockSpec((D_MODEL, tf), lambda i, f: (0, f)),
                  pl.BlockSpec((tf, D_MODEL), lambda i, f: (f, 0))],
        out_specs=rows,
        out_shape=jax.ShapeDtypeStruct((m, D_MODEL), F32),
        scratch_shapes=[pltpu.VMEM((tm, D_MODEL), BF16)],
        compiler_params=_cparams(("parallel", "arbitrary")),
        name="ffn",
    )(x, g, wg, wu, wd)


def _router_kernel(x_ref, g_ref, wr_ref, comb_ref, sel_ref, cnt_ref):
    @pl.when(pl.program_id(0) == 0)
    def _():
        cnt_ref[...] = jnp.zeros_like(cnt_ref)

    h = _rmsnorm(x_ref[...], g_ref[...])
    logits = _dot_f32(h, wr_ref[...])
    lane = lax.broadcasted_iota(jnp.int32, logits.shape, 1)
    logits = jnp.where(lane < N_EXPERTS, logits, -jnp.inf)
    e = jnp.exp(logits - jnp.max(logits, axis=-1, keepdims=True))
    probs = e / jnp.sum(e, axis=-1, keepdims=True)
    p1 = jnp.max(probs, axis=-1, keepdims=True)
    i1 = jnp.min(jnp.where(probs == p1, lane, LANE), axis=-1, keepdims=True)
    rest = jnp.where(lane == i1, -1.0, probs)
    p2 = jnp.max(rest, axis=-1, keepdims=True)
    i2 = jnp.min(jnp.where(rest == p2, lane, LANE), axis=-1, keepdims=True)
    tot = p1 + p2
    comb_ref[...] = jnp.where(lane == i1, p1 / tot, 0.0) + jnp.where(lane == i2, p2 / tot, 0.0)
    sel = jnp.where((lane == i1) | (lane == i2), 1.0, 0.0)
    sel_ref[...] = sel
    cnt_ref[...] += jnp.sum(sel, axis=0, keepdims=True)


def _router(x, g, wr_pad):
    m = x.shape[0]
    tm = min(m, 512)
    rows = pl.BlockSpec((tm, LANE), lambda i: (i, 0))
    return pl.pallas_call(
        _router_kernel,
        grid=(m // tm,),
        in_specs=[pl.BlockSpec((tm, D_MODEL), lambda i: (i, 0)), _const_spec(g), _const_spec(wr_pad)],
        out_specs=[rows, rows, pl.BlockSpec((1, LANE), lambda i: (0, 0))],
        out_shape=[jax.ShapeDtypeStruct((m, LANE), F32), jax.ShapeDtypeStruct((m, LANE), F32),
                   jax.ShapeDtypeStruct((1, LANE), F32)],
        compiler_params=_cparams(("arbitrary",)),
        name="router",
    )(x, g, wr_pad)


def _positions_kernel(sel_ref, comb_ref, off_ref, tril_ref, pa_ref, pb_ref, wts_ref, carry):
    @pl.when(pl.program_id(0) == 0)
    def _():
        carry[...] = jnp.zeros_like(carry)

    sel = sel_ref[...]
    lane = lax.broadcasted_iota(jnp.int32, sel.shape, 1)
    rank = _dot(tril_ref[...], sel.astype(BF16)) + carry[...]
    pos = off_ref[...] + rank
    picked = sel > 0.0
    ia = jnp.min(jnp.where(picked, lane, LANE), axis=-1, keepdims=True)
    ib = jnp.max(jnp.where(picked, lane, -1), axis=-1, keepdims=True)
    ones = jnp.ones((8, LANE), F32)
    pa_ref[0] = _dot_nt_f32(ones, jnp.where(lane == ia, pos, 0.0))[0:1].astype(jnp.int32)
    pb_ref[0] = _dot_nt_f32(ones, jnp.where(lane == ib, pos, 0.0))[0:1].astype(jnp.int32)
    comb = comb_ref[...]
    wa = jnp.sum(jnp.where(lane == ia, comb, 0.0), axis=-1, keepdims=True)
    wb = jnp.sum(jnp.where(lane == ib, comb, 0.0), axis=-1, keepdims=True)
    wts_ref[...] = jnp.where(lane == 0, wa, 0.0) + jnp.where(lane == 1, wb, 0.0)
    carry[...] += jnp.sum(sel, axis=0, keepdims=True)


def _positions(sel, comb, off):
    m = sel.shape[0]
    tp = min(m, 512)
    tril = jnp.asarray(np.tril(np.ones((tp, tp), np.float32), -1)).astype(BF16)
    rows = pl.BlockSpec((tp, LANE), lambda i: (i, 0))
    prow = pl.BlockSpec((1, 1, tp), lambda i: (i, 0, 0))
    pa, pb, wts = pl.pallas_call(
        _positions_kernel,
        grid=(m // tp,),
        in_specs=[rows, rows, _const_spec(off), _const_spec(tril)],
        out_specs=[prow, prow, rows],
        out_shape=[jax.ShapeDtypeStruct((m // tp, 1, tp), jnp.int32),
                   jax.ShapeDtypeStruct((m // tp, 1, tp), jnp.int32),
                   jax.ShapeDtypeStruct((m, LANE), F32)],
        scratch_shapes=[pltpu.VMEM((1, LANE), F32)],
        compiler_params=_cparams(("arbitrary",)),
        name="moe_positions",
    )(sel, comb, off, tril)
    return pa.reshape(m), pb.reshape(m), wts


def _dispatch_kernel(pa_ref, pb_ref, x_hbm, zeros_hbm, xs_hbm, sem, *, ts):
    del zeros_hbm
    i = pl.program_id(0)

    def wait_rows(rows):
        pltpu.make_async_copy(xs_hbm.at[pl.ds(0, rows), :], xs_hbm.at[pl.ds(0, rows), :], sem).wait()

    def body(t, c):
        src = x_hbm.at[pl.ds(i * ts + t, 1), :]
        pltpu.make_async_copy(src, xs_hbm.at[pl.ds(pa_ref[t], 1), :], sem).start()
        pltpu.make_async_copy(src, xs_hbm.at[pl.ds(pb_ref[t], 1), :], sem).start()
        return c

    lax.fori_loop(0, ts, body, 0)

    @pl.when(i > 0)
    def _():
        wait_rows(2 * ts)

    @pl.when(i == pl.num_programs(0) - 1)
    def _():
        wait_rows(2 * ts)


def _dispatch(x, pa, pb, m_pad):
    m = x.shape[0]
    ts = min(m, 512)
    zeros = jnp.zeros((m_pad, D_MODEL), F32)
    smem = pl.BlockSpec((ts,), lambda i: (i,), memory_space=pltpu.SMEM)
    return pl.pallas_call(
        functools.partial(_dispatch_kernel, ts=ts),
        grid=(m // ts,),
        in_specs=[smem, smem, pl.BlockSpec(memory_space=pl.ANY), pl.BlockSpec(memory_space=pl.ANY)],
        out_specs=pl.BlockSpec(memory_space=pl.ANY),
        out_shape=jax.ShapeDtypeStruct((m_pad, D_MODEL), F32),
        scratch_shapes=[pltpu.SemaphoreType.DMA(())],
        input_output_aliases={3: 0},
        compiler_params=_cparams(("arbitrary",)),
        name="moe_dispatch",
    )(pa, pb, x, zeros)


def _experts_kernel(te_ref, nu_ref, x_ref, g_ref, wg_ref, wu_ref, wd_ref, o_ref, *, tf, nf):
    del te_ref
    i = pl.program_id(0)

    @pl.when(i < nu_ref[0])
    def _():
        h = _rmsnorm(x_ref[...], g_ref[...]).astype(BF16)
        acc = None
        for f in range(nf):
            sl = slice(f * tf, (f + 1) * tf)
            act = (_silu(_dot(h, wg_ref[0, :, sl])) * _dot(h, wu_ref[0, :, sl])).astype(BF16)
            part = _dot(act, wd_ref[0, sl, :])
            acc = part if acc is None else acc + part
        o_ref[...] = acc

    @pl.when(i >= nu_ref[0])
    def _():
        o_ref[...] = jnp.zeros_like(o_ref)


def _experts(xs, g, tile_expert, n_used, wg, wu, wd, tm):
    m_pad = xs.shape[0]
    d_exp = wg.shape[2]
    nf = 4
    tf = d_exp // nf
    rows = pl.BlockSpec((tm, D_MODEL), lambda i, te, nu: (i, 0))
    once = pl.Buffered(1)
    grid_spec = pltpu.PrefetchScalarGridSpec(
        num_scalar_prefetch=2,
        grid=(m_pad // tm,),
        in_specs=[rows, pl.BlockSpec((1, D_MODEL), lambda i, te, nu: (0, 0)),
                  pl.BlockSpec((1, D_MODEL, d_exp), lambda i, te, nu: (te[i], 0, 0), pipeline_mode=once),
                  pl.BlockSpec((1, D_MODEL, d_exp), lambda i, te, nu: (te[i], 0, 0), pipeline_mode=once),
                  pl.BlockSpec((1, d_exp, D_MODEL), lambda i, te, nu: (te[i], 0, 0), pipeline_mode=once)],
        out_specs=rows,
    )
    return pl.pallas_call(
        functools.partial(_experts_kernel, tf=tf, nf=nf),
        grid_spec=grid_spec,
        out_shape=jax.ShapeDtypeStruct((m_pad, D_MODEL), F32),
        compiler_params=_cparams(("arbitrary",)),
        name="moe_experts",
    )(tile_expert, n_used, xs, g, wg, wu, wd)


def _combine_kernel(pa_ref, pb_ref, pan_ref, pbn_ref, x_ref, wts_ref, gf_ref, y_hbm, o_ref, bufa, bufb, sem,
                    *, tc, final_norm):
    i = pl.program_id(0)
    slot = i % 2

    def issue(par, pbr, s):
        def body(t, c):
            pltpu.make_async_copy(y_hbm.at[pl.ds(par[t], 1), :], bufa.at[s, pl.ds(t, 1), :], sem.at[s]).start()
            pltpu.make_async_copy(y_hbm.at[pl.ds(pbr[t], 1), :], bufb.at[s, pl.ds(t, 1), :], sem.at[s]).start()
            return c
        lax.fori_loop(0, tc, body, 0)

    @pl.when(i == 0)
    def _():
        issue(pa_ref, pb_ref, 0)

    @pl.when(i + 1 < pl.num_programs(0))
    def _():
        issue(pan_ref, pbn_ref, 1 - slot)

    pltpu.make_async_copy(y_hbm.at[pl.ds(0, tc), :], bufa.at[slot], sem.at[slot]).wait()
    pltpu.make_async_copy(y_hbm.at[pl.ds(0, tc), :], bufb.at[slot], sem.at[slot]).wait()
    wts = wts_ref[...]
    out = x_ref[...] + wts[:, 0:1] * bufa[slot] + wts[:, 1:2] * bufb[slot]
    if final_norm:
        out = _rmsnorm(out, gf_ref[...])
    o_ref[...] = out


def _combine(x, y, pa, pb, wts, g_final, final_norm):
    m = x.shape[0]
    tc = min(m, 256)
    n = m // tc
    cur = pl.BlockSpec((tc,), lambda i: (i,), memory_space=pltpu.SMEM)
    nxt = pl.BlockSpec((tc,), lambda i: (jnp.minimum(i + 1, n - 1),), memory_space=pltpu.SMEM)
    rows = pl.BlockSpec((tc, D_MODEL), lambda i: (i, 0))
    return pl.pallas_call(
        functools.partial(_combine_kernel, tc=tc, final_norm=final_norm),
        grid=(n,),
        in_specs=[cur, cur, nxt, nxt, rows, pl.BlockSpec((tc, LANE), lambda i: (i, 0)), _const_spec(g_final),
                  pl.BlockSpec(memory_space=pl.ANY)],
        out_specs=rows,
        out_shape=jax.ShapeDtypeStruct((m, D_MODEL), F32),
        scratch_shapes=[pltpu.VMEM((2, tc, D_MODEL), F32), pltpu.VMEM((2, tc, D_MODEL), F32),
                        pltpu.SemaphoreType.DMA((2,))],
        compiler_params=_cparams(("arbitrary",)),
        name="moe_combine",
    )(pa, pb, pa, pb, x, wts, g_final, y)


def _moe_routed(x, g, wr_pad, wg, wu, wd, g_final, final_norm):
    m = x.shape[0]
    tm = min(MOE_TM, m)
    comb, sel, cnt = _router(x, g, wr_pad)
    counts = cnt[0, :N_EXPERTS].astype(jnp.int32)
    tiles = (counts + tm - 1) // tm
    ends = jnp.cumsum(tiles)
    off = jnp.zeros((1, LANE), F32).at[0, :N_EXPERTS].set(((ends - tiles) * tm).astype(F32))
    n_tiles = (2 * m) // tm + N_EXPERTS
    tile_expert = jnp.minimum(jnp.searchsorted(ends, jnp.arange(n_tiles, dtype=jnp.int32), side="right"),
                              N_EXPERTS - 1).astype(jnp.int32)
    n_used = ends[-1:].astype(jnp.int32)
    pa, pb, wts = _positions(sel, comb, off)
    xs = _dispatch(x, pa, pb, n_tiles * tm)
    y = _experts(xs, g, tile_expert, n_used, wg, wu, wd, tm)
    return _combine(x, y, pa, pb, wts, g_final, final_norm)


def _moe_dense_kernel(x_ref, g_ref, comb_ref, wg_ref, wu_ref, wd_ref, o_ref, h_scr):
    e = pl.program_id(1)
    f = pl.program_id(2)

    @pl.when((e == 0) & (f == 0))
    def _():
        x = x_ref[...]
        h_scr[...] = _rmsnorm(x, g_ref[...]).astype(BF16)
        o_ref[...] = x

    h = h_scr[...]
    comb = comb_ref[...]
    lane = lax.broadcasted_iota(jnp.int32, comb.shape, 1)
    ce = jnp.sum(jnp.where(lane == e, comb, 0.0), axis=-1, keepdims=True)
    act = (_silu(_dot(h, wg_ref[0])) * _dot(h, wu_ref[0])).astype(BF16)
    o_ref[...] += ce * _dot(act, wd_ref[0])


def _moe_dense(x, g, comb, wg, wu, wd):
    m = x.shape[0]
    d_exp = wg.shape[2]
    tm = min(m, 512)
    tf = d_exp // 4
    rows = pl.BlockSpec((tm, D_MODEL), lambda i, e, f: (i, 0))
    return pl.pallas_call(
        _moe_dense_kernel,
        grid=(m // tm, N_EXPERTS, d_exp // tf),
        in_specs=[rows, pl.BlockSpec((1, D_MODEL), lambda i, e, f: (0, 0)),
                  pl.BlockSpec((tm, LANE), lambda i, e, f: (i, 0)),
                  pl.BlockSpec((1, D_MODEL, tf), lambda i, e, f: (e, 0, f)),
                  pl.BlockSpec((1, D_MODEL, tf), lambda i, e, f: (e, 0, f)),
                  pl.BlockSpec((1, tf, D_MODEL), lambda i, e, f: (e, f, 0))],
        out_specs=rows,
        out_shape=jax.ShapeDtypeStruct((m, D_MODEL), F32),
        scratch_shapes=[pltpu.VMEM((tm, D_MODEL), BF16)],
        compiler_params=_cparams(("parallel", "arbitrary", "arbitrary")),
        name="moe_dense",
    )(x, g, comb, wg, wu, wd)


def _final_norm_kernel(x_ref, g_ref, o_ref):
    o_ref[...] = _rmsnorm(x_ref[...], g_ref[...])


def _final_norm(x, g):
    m = x.shape[0]
    tm = min(m, 1024)
    rows = pl.BlockSpec((tm, D_MODEL), lambda i: (i, 0))
    return pl.pallas_call(
        _final_norm_kernel,
        grid=(m // tm,),
        in_specs=[rows, _const_spec(g)],
        out_specs=rows,
        out_shape=jax.ShapeDtypeStruct((m, D_MODEL), F32),
        compiler_params=_cparams(("parallel",)),
        name="final_norm",
    )(x, g)


def _prep_w_in(w):
    main = w[:, :Z_MAIN]
    ff = jnp.pad(w[:, Z_MAIN:Z_MAIN + H], ((0, 0), (0, LANE - H)))
    gates = w[:, Z_MAIN + H:]
    return jnp.concatenate([main, gates], axis=1).astype(BF16), ff.astype(BF16)


def _rope_tables(pos):
    half = HD // 2
    inv = ROPE_BASE ** (-jnp.arange(half, dtype=F32) / half)
    ang = pos.astype(F32)[:, None] * inv[None, :]
    cos = jnp.cos(ang)
    sin = jnp.sin(ang)
    cos512 = jnp.tile(jnp.concatenate([cos, cos], axis=1), (1, H))
    sin512 = jnp.tile(jnp.concatenate([-sin, sin], axis=1), (1, H))
    return cos512, sin512


def kernel(x_prompt, x_sample, state_ret, state_conv, cache_k, cache_v, cache_logf, page_table, g_mix, w_in,
           b_forget, conv_w, w_ret_o, w_conv_o, w_fox_o, w_o, g_ffn, w_ffn_gate, w_ffn_up, w_ffn_down, w_router,
           w_exp_gate, w_exp_up, w_exp_down, g_final):
    b, l, _ = x_prompt.shape
    bd, t, _ = x_sample.shape
    assert t == 1, "decode group carries one new position per sequence"
    depth = w_in.shape[0]
    n_pages = page_table.shape[1]
    n_phys = cache_k.shape[1]

    log_gamma = jnp.log(1.0 - jnp.exp2(-5.0 - jnp.arange(H, dtype=F32)))
    tables = _ret_tables(log_gamma)
    gamma = jnp.exp(log_gamma)
    cos_p, sin_p = _rope_tables(jnp.arange(l))
    cos_s, sin_s = _rope_tables(n_pages * PAGE + jnp.arange(1))
    clf_flat = cache_logf.reshape(depth, n_phys, 1, FLAT)
    g_fin = g_final.reshape(1, D_MODEL)

    xp = x_prompt.reshape(b * l, D_MODEL)
    xs = x_sample.reshape(bd, D_MODEL)
    outs = {k: [] for k in ("ret_p", "ret_s", "buf_p", "buf_s", "kp", "vp", "lfp", "ks", "vs", "lfs")}
    for layer in range(depth):
        last = layer == depth - 1
        w_in_l, w_ff_l = _prep_w_in(w_in[layer])
        g_l = g_mix[layer].reshape(1, D_MODEL)
        b_pad = jnp.pad(b_forget[layer], (0, LANE - H)).reshape(1, LANE)
        wr_o = w_ret_o[layer].astype(BF16)
        wc_o = w_conv_o[layer].astype(BF16)
        wf_o = w_fox_o[layer].astype(BF16)
        wo = w_o[layer].astype(BF16)
        cw = conv_w[layer]

        z, zb, ff = _inproj(xp, g_l, w_in_l, w_ff_l)
        o_ret, st = _ret_prompt(z, cos_p, sin_p, tables, b, l)
        lf, bias = _forget_prompt(ff, b_pad, b, l)
        o_fox = _fox_prompt(zb, bias.reshape(b, H // 2, 2, l), b, l)
        xp, buf = _merge_prompt(xp, z, o_ret, o_fox, cw, wr_o, wc_o, wf_o, wo, b, l)
        outs["ret_p"].append(st)
        outs["buf_p"].append(buf)
        outs["kp"].append(z[:, _FK * D_BR:(_FK + 1) * D_BR].reshape(b, l, H, HD))
        outs["vp"].append(z[:, _FV * D_BR:(_FV + 1) * D_BR].reshape(b, l, H, HD))
        outs["lfp"].append(lf.reshape(b, l, H))

        zs, _, ffs = _inproj(xs, g_l, w_in_l, w_ff_l)
        seg = lambda j: zs[:, j * D_BR:(j + 1) * D_BR]
        o_ret_s, st_s = _ret_step(zs, state_ret[layer], cos_s, sin_s, gamma)
        lf_s = _forget_step(ffs, b_pad)[:, :H]
        o_fox_s = _fox_decode(page_table, seg(_FQ).reshape(bd, H, HD), seg(_FK).reshape(bd, H, HD),
                              seg(_FV).reshape(bd, H, HD), jnp.tile(lf_s, (1, PAGE)).reshape(bd, 1, FLAT),
                              cache_k, cache_v, clf_flat, layer).reshape(bd, D_BR)
        xs, u_s = _merge_step(xs, zs, o_ret_s, o_fox_s, state_conv[layer, :, 0], state_conv[layer, :, 1],
                              cw, wr_o, wc_o, wf_o, wo)
        outs["ret_s"].append(st_s)
        outs["buf_s"].append(jnp.stack([state_conv[layer, :, 1], u_s], axis=1))
        outs["ks"].append(seg(_FK).reshape(bd, 1, H, HD))
        outs["vs"].append(seg(_FV).reshape(bd, 1, H, HD))
        outs["lfs"].append(lf_s.reshape(bd, 1, H))

        g_f = g_ffn[layer].reshape(1, D_MODEL)
        mi = layer // 2
        if layer % 2 == 0:
            wg = w_ffn_gate[mi].astype(BF16)
            wu = w_ffn_up[mi].astype(BF16)
            wd = w_ffn_down[mi].astype(BF16)
            xp = _ffn(xp, g_f, wg, wu, wd)
            xs = _ffn(xs, g_f, wg, wu, wd)
        else:
            wr_pad = jnp.pad(w_router[mi], ((0, 0), (0, LANE - N_EXPERTS)))
            wg = w_exp_gate[mi].astype(BF16)
            wu = w_exp_up[mi].astype(BF16)
            wd = w_exp_down[mi].astype(BF16)
            xp = _moe_routed(xp, g_f, wr_pad, wg, wu, wd, g_fin, last)
            xs = _moe_dense(xs, g_f, _router(xs, g_f, wr_pad)[0], wg, wu, wd)
        if last and layer % 2 == 0:
            xp = _final_norm(xp, g_fin)

    y_prompt = xp.reshape(b, l, D_MODEL)
    y_sample = _final_norm(xs, g_fin).reshape(bd, 1, D_MODEL)
    st = lambda k: jnp.stack(outs[k])
    return (y_prompt, y_sample, st("ret_p"), st("ret_s"), st("buf_p"), st("buf_s"),
            st("kp"), st("vp"), st("lfp"), st("ks"), st("vs"), st("lfs"))
```

```python
import functools

import jax
import jax.numpy as jnp
import numpy as np
from jax import lax
from jax.experimental import pallas as pl
from jax.experimental.pallas import tpu as pltpu

F32 = jnp.float32
BF16 = jnp.bfloat16

D_MODEL = 1024
H = 8
HD = 64
D_BR = H * HD
RET_CHUNK = 128
PAGE = 128
DEC_PAGES = 16
N_EXPERTS = 8
ROPE_BASE = 10000.0
EPS = 1e-6
LOG2E = 1.4426950408889634
LANE = 128
VMEM_LIMIT = 56 * 1024 * 1024

_RQ, _RK, _RV, _RG, _CB, _CC, _CX, _FQ, _FK, _FV = range(10)
Z_MAIN = 10 * D_BR
Z_COLS = Z_MAIN + 3 * D_MODEL
Z_TN = 2048
FOX_TQ = 1024
FOX_TK = 1024
FOX_TD = 512
RET_ROWS = 512
MOE_TM = 512


def _cparams(sem):
    return pltpu.CompilerParams(dimension_semantics=sem, vmem_limit_bytes=VMEM_LIMIT)


def _rmsnorm(x, g):
    return x * lax.rsqrt(jnp.mean(x * x, axis=-1, keepdims=True) + EPS) * g


def _dot(a, b):
    return jnp.dot(a, b, preferred_element_type=F32)


def _dot_nt(a, b):
    return lax.dot_general(a, b, (((1,), (1,)), ((), ())), preferred_element_type=F32)


def _dot_tn(a, b):
    return lax.dot_general(a, b, (((0,), (0,)), ((), ())), preferred_element_type=F32)


def _dot_f32(a, b):
    return jnp.dot(a, b, preferred_element_type=F32, precision=lax.Precision.HIGHEST)


def _dot_nt_f32(a, b):
    return lax.dot_general(a, b, (((1,), (1,)), ((), ())), preferred_element_type=F32,
                           precision=lax.Precision.HIGHEST)


def _sigmoid(x):
    return 1.0 / (1.0 + jnp.exp(-x))


def _silu(x):
    return x * _sigmoid(x)


def _log_sigmoid(x):
    return jnp.minimum(x, 0.0) - jnp.log(1.0 + jnp.exp(-jnp.abs(x)))


def _const_spec(a):
    n = a.ndim
    return pl.BlockSpec(a.shape, lambda *_, n=n: (0,) * n)


def _inproj_kernel(x_ref, g_ref, w_ref, wf_ref, z_ref, zb_ref, ff_ref, h_scr):
    j = pl.program_id(1)

    @pl.when(j == 0)
    def _():
        h = _rmsnorm(x_ref[...], g_ref[...]).astype(BF16)
        h_scr[...] = h
        ff_ref[...] = _dot(h, wf_ref[...])

    z = _dot(h_scr[...], w_ref[...])
    z_ref[...] = z

    @pl.when(j == 1)
    def _():
        zb_ref[:, 0:D_BR] = (z[:, 3 * D_BR:] * (HD ** -0.5 * LOG2E)).astype(BF16)

    @pl.when(j == 2)
    def _():
        zb_ref[:, D_BR:] = z[:, :2 * D_BR].astype(BF16)


def _inproj(x, g, w, w_ff):
    m = x.shape[0]
    tm = min(m, 1024)
    assert Z_COLS // Z_TN == 4 and _FQ * D_BR == Z_TN + 3 * D_BR and _FK * D_BR == 2 * Z_TN
    return pl.pallas_call(
        _inproj_kernel,
        grid=(m // tm, Z_COLS // Z_TN),
        in_specs=[
            pl.BlockSpec((tm, D_MODEL), lambda i, j: (i, 0)),
            pl.BlockSpec((1, D_MODEL), lambda i, j: (0, 0)),
            pl.BlockSpec((D_MODEL, Z_TN), lambda i, j: (0, j)),
            pl.BlockSpec((D_MODEL, LANE), lambda i, j: (0, 0)),
        ],
        out_specs=[pl.BlockSpec((tm, Z_TN), lambda i, j: (i, j)),
                   pl.BlockSpec((tm, 3 * D_BR), lambda i, j: (i, 0)),
                   pl.BlockSpec((tm, LANE), lambda i, j: (i, 0))],
        out_shape=[jax.ShapeDtypeStruct((m, Z_COLS), F32),
                   jax.ShapeDtypeStruct((m, 3 * D_BR), BF16),
                   jax.ShapeDtypeStruct((m, LANE), F32)],
        scratch_shapes=[pltpu.VMEM((tm, D_MODEL), BF16)],
        compiler_params=_cparams(("parallel", "arbitrary")),
        name="inproj",
    )(x, g, w, w_ff)


def _rope(x, cos, sin_signed):
    lane = lax.broadcasted_iota(jnp.int32, x.shape, 1)
    first = (lane % HD) < (HD // 2)
    n = x.shape[1]
    partner = jnp.where(first, pltpu.roll(x, n - HD // 2, 1), pltpu.roll(x, HD // 2, 1))
    return x * cos + partner * sin_signed


def _ret_prompt_kernel(rq_ref, rk_ref, rv_ref, rg_ref, cos_ref, sin_ref, dec_ref, qd_ref, kd_ref, sd_ref,
                       o_ref, st_ref, s_scr, *, chunks):
    c = pl.program_id(1)

    @pl.when(c == 0)
    def _():
        s_scr[...] = jnp.zeros_like(s_scr)

    low = lax.broadcasted_iota(jnp.int32, (RET_CHUNK, LANE), 1) < HD
    row = lax.broadcasted_iota(jnp.int32, (LANE, LANE), 0)
    col = lax.broadcasted_iota(jnp.int32, (LANE, LANE), 1)
    same_head = (row < HD) == (col < HD)
    states = [s_scr[p] for p in range(H // 2)]
    for ch in range(chunks):
        rs = slice(ch * RET_CHUNK, (ch + 1) * RET_CHUNK)
        cos = cos_ref[rs, :]
        sin = sin_ref[rs, :]
        q = _rope(rq_ref[rs, :], cos, sin)
        k = _rope(rk_ref[rs, :], cos, sin) * (HD ** -0.5)
        v = rv_ref[rs, :]
        for p in range(H // 2):
            sl = slice(p * LANE, (p + 1) * LANE)
            qp = q[:, sl]
            kp = k[:, sl].astype(BF16)
            vp = v[:, sl].astype(BF16)
            inner = []
            for hh in range(2):
                qm = jnp.where(low if hh == 0 else jnp.logical_not(low), qp, 0.0).astype(BF16)
                sc = _dot_nt(qm, kp) * dec_ref[2 * p + hh]
                inner.append(_dot(sc.astype(BF16), vp))
            s_old = states[p]
            cross = _dot(qp.astype(BF16), s_old.astype(BF16)) * qd_ref[:, sl]
            o = jnp.where(low, inner[0], inner[1]) + cross
            kdec = (k[:, sl] * kd_ref[:, sl]).astype(BF16)
            states[p] = sd_ref[p] * s_old + jnp.where(same_head, _dot_tn(kdec, vp), 0.0)
            oo = o * o
            ms = jnp.where(low, jnp.sum(jnp.where(low, oo, 0.0), axis=-1, keepdims=True),
                           jnp.sum(jnp.where(low, 0.0, oo), axis=-1, keepdims=True)) * (1.0 / HD)
            o_ref[rs, sl] = (o * lax.rsqrt(ms + EPS) * _silu(rg_ref[rs, sl])).astype(BF16)
    for p in range(H // 2):
        s_scr[p] = states[p]

    @pl.when(c == pl.num_programs(1) - 1)
    def _():
        for p in range(H // 2):
            st_ref[0, 2 * p] = states[p][:HD, :HD]
            st_ref[0, 2 * p + 1] = states[p][HD:, HD:]


def _ret_tables(log_gamma):
    i = jnp.arange(RET_CHUNK, dtype=F32)
    diff = i[:, None] - i[None, :]
    decay = jnp.where(diff >= 0, jnp.exp(log_gamma[:, None, None] * jnp.maximum(diff, 0.0)), 0.0)
    q_decay = jnp.exp(log_gamma[:, None] * (i + 1.0)[None, :]).T
    k_decay = jnp.exp(log_gamma[:, None] * (RET_CHUNK - 1.0 - i)[None, :]).T
    s_decay = jnp.exp(log_gamma * RET_CHUNK)
    qd = jnp.repeat(q_decay, HD, axis=1)
    kd = jnp.repeat(k_decay, HD, axis=1)
    sd = jnp.broadcast_to(jnp.repeat(s_decay, HD).reshape(H // 2, LANE, 1), (H // 2, LANE, LANE))
    return decay, qd, kd, sd


def _ret_prompt(z, cos, sin, tables, b, l):
    decay, qd, kd, sd = tables
    tr = min(l, RET_ROWS)
    nc = l // tr
    blk = lambda j: pl.BlockSpec((tr, D_BR), lambda bi, c, j=j: (bi * nc + c, j))
    tab = pl.BlockSpec((tr, D_BR), lambda bi, c: (c, 0))
    return pl.pallas_call(
        functools.partial(_ret_prompt_kernel, chunks=tr // RET_CHUNK),
        grid=(b, nc),
        in_specs=[blk(_RQ), blk(_RK), blk(_RV), blk(_RG), tab, tab,
                  _const_spec(decay), _const_spec(qd), _const_spec(kd), _const_spec(sd)],
        out_specs=[pl.BlockSpec((tr, D_BR), lambda bi, c: (bi * nc + c, 0)),
                   pl.BlockSpec((1, H, HD, HD), lambda bi, c: (bi, 0, 0, 0))],
        out_shape=[jax.ShapeDtypeStruct((b * l, D_BR), BF16),
                   jax.ShapeDtypeStruct((b, H, HD, HD), F32)],
        scratch_shapes=[pltpu.VMEM((H // 2, LANE, LANE), F32)],
        compiler_params=_cparams(("parallel", "arbitrary")),
        name="ret_prompt",
    )(z, z, z, z, cos, sin, decay, qd, kd, sd)


def _ret_step_kernel(q_ref, k_ref, v_ref, g_ref, s_ref, cos_ref, sin_ref, gam_ref, grow_ref, o_ref, sn_ref):
    cos = cos_ref[...]
    sin = sin_ref[...]
    q = _rope(q_ref[0], cos, sin).astype(BF16)
    k = (_rope(k_ref[0], cos, sin) * (HD ** -0.5)).astype(BF16)
    own = (lax.broadcasted_iota(jnp.int32, (H, D_BR), 1) // HD) == lax.broadcasted_iota(jnp.int32, (H, D_BR), 0)
    q_bd = jnp.where(own, jnp.broadcast_to(q.astype(F32), (H, D_BR)), 0.0)
    k_bd = jnp.where(own, jnp.broadcast_to(k.astype(F32), (H, D_BR)), 0.0)
    v = v_ref[0].astype(BF16)
    sc = jnp.sum(q_bd * k.astype(F32), axis=-1, keepdims=True)
    inner = sc.astype(BF16).astype(F32) * v.astype(F32)
    s_old = s_ref[0]
    cross = _dot(q_bd.astype(BF16), s_old.astype(BF16)) * gam_ref[...]
    o = inner + cross
    o = o * lax.rsqrt(jnp.mean(o * o, axis=-1, keepdims=True) + EPS)
    o_ref[0] = o * _silu(g_ref[0])
    sn_ref[0] = grow_ref[...] * s_old + _dot_tn(k_bd.astype(BF16), v)


def _ret_step(zs, state, cos512, sin512, gamma):
    bd = zs.shape[0]
    seg = lambda j: zs[:, j * D_BR:(j + 1) * D_BR]
    s2 = state.reshape(bd, D_BR, HD)
    gam = gamma.reshape(H, 1)
    grow = jnp.broadcast_to(jnp.repeat(gamma, HD)[:, None], (D_BR, HD))
    flat = pl.BlockSpec((1, 1, D_BR), lambda i: (i, 0, 0))
    r3 = pl.BlockSpec((1, H, HD), lambda i: (i, 0, 0))
    s3 = pl.BlockSpec((1, D_BR, HD), lambda i: (i, 0, 0))
    o, sn = pl.pallas_call(
        _ret_step_kernel,
        grid=(bd,),
        in_specs=[flat, flat, r3, r3, s3,
                  _const_spec(cos512), _const_spec(sin512), _const_spec(gam), _const_spec(grow)],
        out_specs=[r3, s3],
        out_shape=[jax.ShapeDtypeStruct((bd, H, HD), F32),
                   jax.ShapeDtypeStruct((bd, D_BR, HD), F32)],
        compiler_params=_cparams(("parallel",)),
        name="ret_step",
    )(seg(_RQ).reshape(bd, 1, D_BR), seg(_RK).reshape(bd, 1, D_BR), seg(_RV).reshape(bd, H, HD),
      seg(_RG).reshape(bd, H, HD), s2, cos512, sin512, gam, grow)
    return o.reshape(bd, D_BR), sn.reshape(bd, H, HD, HD)


def _forget_prompt_kernel(ff_ref, b_ref, tri_ref, lf_ref, bias_ref, carry):
    @pl.when(pl.program_id(1) == 0)
    def _():
        carry[...] = jnp.zeros_like(carry)

    lf = _log_sigmoid(ff_ref[...] + b_ref[...])
    lf_ref[...] = lf[:, :H]
    c = _dot_f32(tri_ref[...], lf) + carry[...]
    bias_ref[0] = c.T[:H, :] * (-LOG2E)
    carry[...] = c[-1:, :]


def _forget_prompt(ff, b_pad, b, l):
    tc = min(l, 512)
    nt = l // tc
    tri = jnp.asarray(np.tril(np.ones((tc, tc), np.float32)))
    return pl.pallas_call(
        _forget_prompt_kernel,
        grid=(b, nt),
        in_specs=[pl.BlockSpec((tc, LANE), lambda bi, t: (bi * nt + t, 0)),
                  _const_spec(b_pad), _const_spec(tri)],
        out_specs=[pl.BlockSpec((tc, H), lambda bi, t: (bi * nt + t, 0)),
                   pl.BlockSpec((1, H, tc), lambda bi, t: (bi, 0, t))],
        out_shape=[jax.ShapeDtypeStruct((b * l, H), F32),
                   jax.ShapeDtypeStruct((b, H, l), F32)],
        scratch_shapes=[pltpu.VMEM((1, LANE), F32)],
        compiler_params=_cparams(("parallel", "arbitrary")),
        name="forget_prompt",
    )(ff, b_pad, tri)


def _forget_step_kernel(ff_ref, b_ref, lf_ref):
    lf_ref[...] = _log_sigmoid(ff_ref[...] + b_ref[...])


def _forget_step(ff, b_pad):
    m = ff.shape[0]
    full = pl.BlockSpec((m, LANE), lambda i: (0, 0))
    return pl.pallas_call(
        _forget_step_kernel,
        grid=(1,),
        in_specs=[full, _const_spec(b_pad)],
        out_specs=full,
        out_shape=jax.ShapeDtypeStruct((m, LANE), F32),
        name="forget_step",
    )(ff, b_pad)


def _fox_prompt_kernel(q_ref, k_ref, v_ref, bias_ref, o_ref, *, tq, tk, td):
    qi = pl.program_id(2)
    zero = jnp.zeros((), BF16)
    one = jnp.ones((), BF16)
    lowq = lax.broadcasted_iota(jnp.int32, (tq, LANE), 1) < HD
    q = q_ref[...]
    qm = (jnp.where(lowq, q, zero), jnp.where(lowq, zero, q))

    def step(ks, width, r0, carry, masked):
        rows = tq - r0
        lowk = lax.broadcasted_iota(jnp.int32, (width, LANE), 1) < HD
        kb = k_ref[pl.ds(ks, width), :]
        vb = v_ref[pl.ds(ks, width), :]
        vh = (jnp.where(lowk, vb, one), jnp.where(lowk, one, vb))
        new = []
        for hh in range(2):
            m_all, acc_all = carry[hh]
            m, acc = m_all[r0:], acc_all[r0:]
            s = _dot_nt(qm[hh][r0:], kb) + bias_ref[0, 0, pl.ds(hh, 1), pl.ds(ks, width)]
            if masked:
                qpos = qi * tq + r0 + lax.broadcasted_iota(jnp.int32, (rows, width), 0)
                kpos = ks + lax.broadcasted_iota(jnp.int32, (rows, width), 1)
                s = jnp.where(kpos <= qpos, s, -jnp.inf)
            m_new = jnp.maximum(m, jnp.max(s, axis=-1, keepdims=True))
            p = jnp.exp2(s - m_new)
            acc = jnp.exp2(m - m_new) * acc + _dot(p.astype(BF16), vh[hh])
            if r0:
                m_new = jnp.concatenate([m_all[:r0], m_new], axis=0)
                acc = jnp.concatenate([acc_all[:r0], acc], axis=0)
            new.append((m_new, acc))
        return tuple(new)

    init = tuple((jnp.full((tq, 1), -jnp.inf, F32), jnp.zeros((tq, LANE), F32)) for _ in range(2))
    n_full = qi * (tq // tk)
    carry = lax.fori_loop(0, n_full, lambda j, c: step(pl.multiple_of(j * tk, tk), tk, 0, c, False), init)
    for d in range(tq // td):
        carry = step(pl.multiple_of(qi * tq + d * td, td), td, d * td, carry, True)
    (_, acc0), (_, acc1) = carry
    out0 = acc0 / acc0[:, HD:HD + 1]
    out1 = acc1 / acc1[:, 0:1]
    o_ref[...] = jnp.where(lowq, out0, out1).astype(BF16)


def _fox_prompt(zb, bias, b, l):
    tq = min(l, FOX_TQ)
    tk = min(tq, FOX_TK)
    td = min(tk, FOX_TD)
    nq = l // tq
    npair = H // 2
    qspec = pl.BlockSpec((tq, LANE), lambda bi, p, qi: (bi * nq + qi, p))
    kspec = pl.BlockSpec((l, LANE), lambda bi, p, qi: (bi, npair + p))
    vspec = pl.BlockSpec((l, LANE), lambda bi, p, qi: (bi, 2 * npair + p))
    return pl.pallas_call(
        functools.partial(_fox_prompt_kernel, tq=tq, tk=tk, td=td),
        grid=(b, npair, nq),
        in_specs=[qspec, kspec, vspec,
                  pl.BlockSpec((1, 1, 2, l), lambda bi, p, qi: (bi, p, 0, 0))],
        out_specs=qspec,
        out_shape=jax.ShapeDtypeStruct((b * l, D_BR), BF16),
        compiler_params=_cparams(("parallel", "parallel", "arbitrary")),
        name="fox_prompt",
    )(zb, zb, zb, bias)


def _fox_decode_kernel(pt_ref, q_ref, kn_ref, vn_ref, lfn_ref, suf_ref, *rest, g):
    del pt_ref
    k_refs = rest[:g]
    v_refs = rest[g:2 * g]
    lf_refs = rest[2 * g:3 * g]
    o_ref = rest[3 * g]
    m_scr, l_scr, acc_scr, cy_scr = rest[3 * g + 1:]
    j = pl.program_id(1)
    lane = lax.broadcasted_iota(jnp.int32, (HD, PAGE), 1)
    qcol = [q_ref[0, h] * (HD ** -0.5 * LOG2E) for h in range(H)]

    @pl.when(j == 0)
    def _():
        m_scr[...] = jnp.concatenate(
            [jnp.sum(qcol[h] * kn_ref[0, h], axis=0, keepdims=True) for h in range(H)], axis=0)
        l_scr[...] = jnp.ones_like(l_scr)
        for h in range(H):
            acc_scr[h] = jnp.where(lane == 0, vn_ref[0, h], 0.0)
        cy_scr[...] = lfn_ref[0] * LOG2E

    lf = jnp.concatenate([lf_refs[t][0, 0] for t in range(g)], axis=0) * LOG2E
    suf = _dot_f32(lf, suf_ref[...])
    tot = jnp.sum(lf, axis=1, keepdims=True)
    cy = cy_scr[...]
    rows = []
    for t in range(g):
        s_t = jnp.concatenate(
            [jnp.sum(k_refs[t][0, 0, h] * qcol[h], axis=0, keepdims=True) for h in range(H)], axis=0)
        rows.append(s_t + suf[t * H:(t + 1) * H] + cy)
        cy = cy + tot[t * H:(t + 1) * H]
    cy_scr[...] = cy
    m_old = m_scr[...]
    m_new = m_old
    for t in range(g):
        m_new = jnp.maximum(m_new, jnp.max(rows[t], axis=1, keepdims=True))
    alpha = jnp.exp2(m_old - m_new)
    ps = [jnp.exp2(rows[t] - m_new) for t in range(g)]
    l_new = alpha * l_scr[...]
    for t in range(g):
        l_new = l_new + jnp.sum(ps[t], axis=1, keepdims=True)
    l_scr[...] = l_new
    m_scr[...] = m_new
    for h in range(H):
        acc = alpha[h:h + 1, :] * acc_scr[h]
        for t in range(g):
            acc = acc + v_refs[t][0, 0, h] * ps[t][h:h + 1, :]
        acc_scr[h] = acc

    @pl.when(j == pl.num_programs(1) - 1)
    def _():
        l = l_scr[...]
        for h in range(H):
            o_ref[0, h] = jnp.sum(acc_scr[h], axis=1, keepdims=True) / l[h:h + 1, :]


def _fox_decode(page_table, qcol, kncol, vncol, lfn, cache_kt, cache_vt, cache_lft, layer):
    bd, n_pages = page_table.shape
    g = DEC_PAGES if n_pages % DEC_PAGES == 0 else 1
    steps = n_pages // g
    jj = np.arange(PAGE)
    suf = jnp.asarray((jj[:, None] > jj[None, :]).astype(np.float32))

    def page(nd):
        def spec(t):
            return lambda b, j, pt, t=t: (layer, pt[b, n_pages - 1 - (j * g + t)]) + (0,) * nd
        return spec

    c4 = pl.BlockSpec((1, H, HD, 1), lambda b, j, pt: (b, 0, 0, 0))
    in_specs = [c4, c4, c4, pl.BlockSpec((1, H, 1), lambda b, j, pt: (b, 0, 0)),
                pl.BlockSpec((PAGE, PAGE), lambda b, j, pt: (0, 0))]
    in_specs += [pl.BlockSpec((1, 1, H, HD, PAGE), page(3)(t)) for t in range(g)]
    in_specs += [pl.BlockSpec((1, 1, H, HD, PAGE), page(3)(t)) for t in range(g)]
    in_specs += [pl.BlockSpec((1, 1, H, PAGE), page(2)(t)) for t in range(g)]
    grid_spec = pltpu.PrefetchScalarGridSpec(
        num_scalar_prefetch=1,
        grid=(bd, steps),
        in_specs=in_specs,
        out_specs=c4,
        scratch_shapes=[pltpu.VMEM((H, 1), F32), pltpu.VMEM((H, 1), F32),
                        pltpu.VMEM((H, HD, PAGE), F32), pltpu.VMEM((H, 1), F32)],
    )
    return pl.pallas_call(
        functools.partial(_fox_decode_kernel, g=g),
        grid_spec=grid_spec,
        out_shape=jax.ShapeDtypeStruct((bd, H, HD, 1), F32),
        compiler_params=_cparams(("parallel", "arbitrary")),
        name="fox_decode",
    )(page_table, qcol, kncol, vncol, lfn, suf, *([cache_kt] * g), *([cache_vt] * g), *([cache_lft] * g))


def _kv_transposed_kernel(*refs):
    depth = len(refs) // 2 - 1
    kt_ref, vt_ref = refs[-2], refs[-1]
    for layer in range(depth):
        kt_ref[layer, 0] = refs[2 * layer][...].T
        vt_ref[layer, 0] = refs[2 * layer + 1][...].T


def _kv_transposed(zs, b, l):
    depth = len(zs)
    m = zs[0].shape[0]
    tm = min(l, 1024)
    tps = l // tm
    zspec = lambda j: pl.BlockSpec((tm, D_BR), lambda i, j=j: (i, j))
    ospec = pl.BlockSpec((depth, 1, D_BR, tm), lambda i: (0, i // tps, 0, i % tps))
    shape = jax.ShapeDtypeStruct((depth, b, D_BR, l), F32)
    return pl.pallas_call(
        _kv_transposed_kernel,
        grid=(m // tm,),
        in_specs=[zspec(_FK), zspec(_FV)] * depth,
        out_specs=[ospec, ospec],
        out_shape=[shape, shape],
        compiler_params=_cparams(("parallel",)),
        name="kv_transposed",
    )(*[z for z in zs for _ in range(2)])


def _merge_math(x, o_ret, y_conv, o_fox, a1, a2, a3, wr_ref, wc_ref, wf_ref, wo_ref):
    merged = (_sigmoid(a1) * _dot(o_ret, wr_ref[...])
              + _sigmoid(a2) * _dot(y_conv, wc_ref[...])
              + _sigmoid(a3) * _dot(o_fox, wf_ref[...]))
    return x + _dot(merged.astype(BF16), wo_ref[...])


def _merge_prompt_kernel(x_ref, oret_ref, ofox_ref, cb_ref, cc_ref, cx_ref, ccp_ref, cxp_ref,
                         a1_ref, a2_ref, a3_ref, cw_ref, wr_ref, wc_ref, wf_ref, wo_ref,
                         xo_ref, buf_ref, *, tm, tiles_per_seq):
    i = pl.program_id(0)
    u = cc_ref[...] * cx_ref[...]
    prev = ccp_ref[...] * cxp_ref[...]
    prev = jnp.where(i % tiles_per_seq == 0, 0.0, prev)
    p1 = prev[7:8, :]
    p2 = prev[6:7, :]
    row = lax.broadcasted_iota(jnp.int32, u.shape, 0)
    u1 = jnp.where(row >= 1, pltpu.roll(u, 1, 0), p1)
    u2 = jnp.where(row >= 2, pltpu.roll(u, 2, 0), jnp.where(row == 1, p1, p2))
    y = u2 * cw_ref[0:1, :] + u1 * cw_ref[1:2, :] + u * cw_ref[2:3, :]
    y_conv = (cb_ref[...] * y).astype(BF16)
    buf_ref[0] = u[tm - 2:, :]
    xo_ref[...] = _merge_math(x_ref[...], oret_ref[...], y_conv, ofox_ref[...],
                              a1_ref[...], a2_ref[...], a3_ref[...], wr_ref, wc_ref, wf_ref, wo_ref)


def _merge_prompt(x, z, o_ret, o_fox, conv_w, w_ret_o, w_conv_o, w_fox_o, w_o, b, l):
    m = b * l
    tm = min(l, 512)
    tps = l // tm
    zb = lambda j: pl.BlockSpec((tm, D_BR), lambda i, j=j: (i, j))
    zprev = lambda j: pl.BlockSpec((8, D_BR), lambda i, j=j: (jnp.maximum(i * (tm // 8) - 1, 0), j))
    za = lambda j: pl.BlockSpec((tm, D_MODEL), lambda i, j=j: (i, Z_MAIN // D_MODEL + j))
    rows512 = pl.BlockSpec((tm, D_BR), lambda i: (i, 0))
    rows = pl.BlockSpec((tm, D_MODEL), lambda i: (i, 0))
    return pl.pallas_call(
        functools.partial(_merge_prompt_kernel, tm=tm, tiles_per_seq=tps),
        grid=(m // tm,),
        in_specs=[rows, rows512, rows512, zb(_CB), zb(_CC), zb(_CX), zprev(_CC), zprev(_CX),
                  za(0), za(1), za(2), _const_spec(conv_w),
                  _const_spec(w_ret_o), _const_spec(w_conv_o), _const_spec(w_fox_o), _const_spec(w_o)],
        out_specs=[rows, pl.BlockSpec((1, 2, D_BR), lambda i: (i // tps, 0, 0))],
        out_shape=[jax.ShapeDtypeStruct((m, D_MODEL), F32),
                   jax.ShapeDtypeStruct((b, 2, D_BR), F32)],
        compiler_params=_cparams(("arbitrary",)),
        name="merge_prompt",
    )(x, o_ret, o_fox, z, z, z, z, z, z, z, z, conv_w, w_ret_o, w_conv_o, w_fox_o, w_o)


def _merge_step_kernel(x_ref, oret_ref, ofox_ref, cb_ref, cc_ref, cx_ref, b0_ref, b1_ref,
                       a1_ref, a2_ref, a3_ref, cw_ref, wr_ref, wc_ref, wf_ref, wo_ref, xo_ref, u_ref):
    u = cc_ref[...] * cx_ref[...]
    y = b0_ref[...] * cw_ref[0:1, :] + b1_ref[...] * cw_ref[1:2, :] + u * cw_ref[2:3, :]
    y_conv = (cb_ref[...] * y).astype(BF16)
    u_ref[...] = u
    xo_ref[...] = _merge_math(x_ref[...], oret_ref[...].astype(BF16), y_conv, ofox_ref[...].astype(BF16),
                              a1_ref[...], a2_ref[...], a3_ref[...], wr_ref, wc_ref, wf_ref, wo_ref)


def _merge_step(x, z, o_ret, o_fox, buf0, buf1, conv_w, w_ret_o, w_conv_o, w_fox_o, w_o):
    m = x.shape[0]
    zb = lambda j: pl.BlockSpec((m, D_BR), lambda i, j=j: (0, j))
    za = lambda j: pl.BlockSpec((m, D_MODEL), lambda i, j=j: (0, Z_MAIN // D_MODEL + j))
    rows512 = pl.BlockSpec((m, D_BR), lambda i: (0, 0))
    rows = pl.BlockSpec((m, D_MODEL), lambda i: (0, 0))
    return pl.pallas_call(
        _merge_step_kernel,
        grid=(1,),
        in_specs=[rows, rows512, rows512, zb(_CB), zb(_CC), zb(_CX), rows512, rows512,
                  za(0), za(1), za(2), _const_spec(conv_w),
                  _const_spec(w_ret_o), _const_spec(w_conv_o), _const_spec(w_fox_o), _const_spec(w_o)],
        out_specs=[rows, rows512],
        out_shape=[jax.ShapeDtypeStruct((m, D_MODEL), F32),
                   jax.ShapeDtypeStruct((m, D_BR), F32)],
        compiler_params=_cparams(("arbitrary",)),
        name="merge_step",
    )(x, o_ret, o_fox, z, z, z, buf0, buf1, z, z, z, conv_w, w_ret_o, w_conv_o, w_fox_o, w_o)


def _ffn_kernel(x_ref, g_ref, wg_ref, wu_ref, wd_ref, o_ref, h_scr):
    f = pl.program_id(1)

    @pl.when(f == 0)
    def _():
        x = x_ref[...]
        h_scr[...] = _rmsnorm(x, g_ref[...]).astype(BF16)
        o_ref[...] = x

    h = h_scr[...]
    act = (_silu(_dot(h, wg_ref[...])) * _dot(h, wu_ref[...])).astype(BF16)
    o_ref[...] += _dot(act, wd_ref[...])


def _ffn(x, g, wg, wu, wd):
    m = x.shape[0]
    d_ff = wg.shape[1]
    tm = min(m, 512)
    tf = d_ff // 2
    rows = pl.BlockSpec((tm, D_MODEL), lambda i, f: (i, 0))
    return pl.pallas_call(
        _ffn_kernel,
        grid=(m // tm, d_ff // tf),
        in_specs=[rows, pl.BlockSpec((1, D_MODEL), lambda i, f: (0, 0)),
                  pl.BlockSpec((D_MODEL, tf), lambda i, f: (0, f)),
                  pl.BlockSpec((D_MODEL, tf), lambda i, f: (0, f)),
                  pl.BlockSpec((tf, D_MODEL), lambda i, f: (f, 0))],
        out_specs=rows,
        out_shape=jax.ShapeDtypeStruct((m, D_MODEL), F32),
        scratch_shapes=[pltpu.VMEM((tm, D_MODEL), BF16)],
        compiler_params=_cparams(("parallel", "arbitrary")),
        name="ffn",
    )(x, g, wg, wu, wd)


def _router_kernel(x_ref, g_ref, wr_ref, comb_ref, sel_ref, cnt_ref):
    @pl.when(pl.program_id(0) == 0)
    def _():
        cnt_ref[...] = jnp.zeros_like(cnt_ref)

    h = _rmsnorm(x_ref[...], g_ref[...])
    logits = _dot_f32(h, wr_ref[...])
    lane = lax.broadcasted_iota(jnp.int32, logits.shape, 1)
    logits = jnp.where(lane < N_EXPERTS, logits, -jnp.inf)
    e = jnp.exp(logits - jnp.max(logits, axis=-1, keepdims=True))
    probs = e / jnp.sum(e, axis=-1, keepdims=True)
    p1 = jnp.max(probs, axis=-1, keepdims=True)
    i1 = jnp.min(jnp.where(probs == p1, lane, LANE), axis=-1, keepdims=True)
    rest = jnp.where(lane == i1, -1.0, probs)
    p2 = jnp.max(rest, axis=-1, keepdims=True)
    i2 = jnp.min(jnp.where(rest == p2, lane, LANE), axis=-1, keepdims=True)
    tot = p1 + p2
    comb_ref[...] = jnp.where(lane == i1, p1 / tot, 0.0) + jnp.where(lane == i2, p2 / tot, 0.0)
    sel = jnp.where((lane == i1) | (lane == i2), 1.0, 0.0)
    sel_ref[...] = sel
    cnt_ref[...] += jnp.sum(sel, axis=0, keepdims=True)


def _router(x, g, wr_pad):
    m = x.shape[0]
    tm = min(m, 512)
    rows = pl.BlockSpec((tm, LANE), lambda i: (i, 0))
    return pl.pallas_call(
        _router_kernel,
        grid=(m // tm,),
        in_specs=[pl.BlockSpec((tm, D_MODEL), lambda i: (i, 0)), _const_spec(g), _const_spec(wr_pad)],
        out_specs=[rows, rows, pl.BlockSpec((1, LANE), lambda i: (0, 0))],
        out_shape=[jax.ShapeDtypeStruct((m, LANE), F32), jax.ShapeDtypeStruct((m, LANE), F32),
                   jax.ShapeDtypeStruct((1, LANE), F32)],
        compiler_params=_cparams(("arbitrary",)),
        name="router",
    )(x, g, wr_pad)


def _positions_kernel(sel_ref, comb_ref, off_ref, tril_ref, pa_ref, pb_ref, wts_ref, carry):
    @pl.when(pl.program_id(0) == 0)
    def _():
        carry[...] = jnp.zeros_like(carry)

    sel = sel_ref[...]
    lane = lax.broadcasted_iota(jnp.int32, sel.shape, 1)
    rank = _dot(tril_ref[...], sel.astype(BF16)) + carry[...]
    pos = off_ref[...] + rank
    picked = sel > 0.0
    ia = jnp.min(jnp.where(picked, lane, LANE), axis=-1, keepdims=True)
    ib = jnp.max(jnp.where(picked, lane, -1), axis=-1, keepdims=True)
    ones = jnp.ones((8, LANE), F32)
    pa_ref[0] = _dot_nt_f32(ones, jnp.where(lane == ia, pos, 0.0))[0:1].astype(jnp.int32)
    pb_ref[0] = _dot_nt_f32(ones, jnp.where(lane == ib, pos, 0.0))[0:1].astype(jnp.int32)
    comb = comb_ref[...]
    wa = jnp.sum(jnp.where(lane == ia, comb, 0.0), axis=-1, keepdims=True)
    wb = jnp.sum(jnp.where(lane == ib, comb, 0.0), axis=-1, keepdims=True)
    wts_ref[...] = jnp.where(lane == 0, wa, 0.0) + jnp.where(lane == 1, wb, 0.0)
    carry[...] += jnp.sum(sel, axis=0, keepdims=True)


def _positions(sel, comb, off):
    m = sel.shape[0]
    tp = min(m, 512)
    tril = jnp.asarray(np.tril(np.ones((tp, tp), np.float32), -1)).astype(BF16)
    rows = pl.BlockSpec((tp, LANE), lambda i: (i, 0))
    prow = pl.BlockSpec((1, 1, tp), lambda i: (i, 0, 0))
    pa, pb, wts = pl.pallas_call(
        _positions_kernel,
        grid=(m // tp,),
        in_specs=[rows, rows, _const_spec(off), _const_spec(tril)],
        out_specs=[prow, prow, rows],
        out_shape=[jax.ShapeDtypeStruct((m // tp, 1, tp), jnp.int32),
                   jax.ShapeDtypeStruct((m // tp, 1, tp), jnp.int32),
                   jax.ShapeDtypeStruct((m, LANE), F32)],
        scratch_shapes=[pltpu.VMEM((1, LANE), F32)],
        compiler_params=_cparams(("arbitrary",)),
        name="moe_positions",
    )(sel, comb, off, tril)
    return pa.reshape(m), pb.reshape(m), wts


def _dispatch_kernel(pa_ref, pb_ref, x_ref, zeros_hbm, xs_hbm, sem, *, ts):
    del zeros_hbm

    def body(t, c):
        src = x_ref.at[pl.ds(t, 1), :]
        pltpu.make_async_copy(src, xs_hbm.at[pl.ds(pa_ref[t], 1), :], sem).start()
        pltpu.make_async_copy(src, xs_hbm.at[pl.ds(pb_ref[t], 1), :], sem).start()
        return c

    lax.fori_loop(0, ts, body, 0)
    pltpu.make_async_copy(xs_hbm.at[pl.ds(0, 2 * ts), :], xs_hbm.at[pl.ds(0, 2 * ts), :], sem).wait()


def _dispatch(x, pa, pb, m_pad):
    m = x.shape[0]
    ts = min(m, 1024)
    zeros = jnp.zeros((m_pad, D_MODEL), F32)
    smem = pl.BlockSpec((ts,), lambda i: (i,), memory_space=pltpu.SMEM)
    return pl.pallas_call(
        functools.partial(_dispatch_kernel, ts=ts),
        grid=(m // ts,),
        in_specs=[smem, smem, pl.BlockSpec((ts, D_MODEL), lambda i: (i, 0)), pl.BlockSpec(memory_space=pl.ANY)],
        out_specs=pl.BlockSpec(memory_space=pl.ANY),
        out_shape=jax.ShapeDtypeStruct((m_pad, D_MODEL), F32),
        scratch_shapes=[pltpu.SemaphoreType.DMA(())],
        input_output_aliases={3: 0},
        compiler_params=_cparams(("arbitrary",)),
        name="moe_dispatch",
    )(pa, pb, x, zeros)


def _experts_kernel(te_ref, nu_ref, x_ref, g_ref, wg_ref, wu_ref, wd_ref, o_ref, *, tf, nf):
    del te_ref
    i = pl.program_id(0)

    @pl.when(i < nu_ref[0])
    def _():
        h = _rmsnorm(x_ref[...], g_ref[...]).astype(BF16)
        acc = None
        for f in range(nf):
            sl = slice(f * tf, (f + 1) * tf)
            act = (_silu(_dot(h, wg_ref[0, :, sl])) * _dot(h, wu_ref[0, :, sl])).astype(BF16)
            part = _dot(act, wd_ref[0, sl, :])
            acc = part if acc is None else acc + part
        o_ref[...] = acc

    @pl.when(i >= nu_ref[0])
    def _():
        o_ref[...] = jnp.zeros_like(o_ref)


def _experts(xs, g, tile_expert, n_used, wg, wu, wd, tm):
    m_pad = xs.shape[0]
    d_exp = wg.shape[2]
    nf = 4
    tf = d_exp // nf
    rows = pl.BlockSpec((tm, D_MODEL), lambda i, te, nu: (i, 0))
    once = pl.Buffered(1)
    grid_spec = pltpu.PrefetchScalarGridSpec(
        num_scalar_prefetch=2,
        grid=(m_pad // tm,),
        in_specs=[rows, pl.BlockSpec((1, D_MODEL), lambda i, te, nu: (0, 0)),
                  pl.BlockSpec((1, D_MODEL, d_exp), lambda i, te, nu: (te[i], 0, 0), pipeline_mode=once),
                  pl.BlockSpec((1, D_MODEL, d_exp), lambda i, te, nu: (te[i], 0, 0), pipeline_mode=once),
                  pl.BlockSpec((1, d_exp, D_MODEL), lambda i, te, nu: (te[i], 0, 0), pipeline_mode=once)],
        out_specs=rows,
    )
    return pl.pallas_call(
        functools.partial(_experts_kernel, tf=tf, nf=nf),
        grid_spec=grid_spec,
        out_shape=jax.ShapeDtypeStruct((m_pad, D_MODEL), F32),
        compiler_params=_cparams(("arbitrary",)),
        name="moe_experts",
    )(tile_expert, n_used, xs, g, wg, wu, wd)


def _combine_kernel(pa_ref, pb_ref, pan_ref, pbn_ref, x_ref, wts_ref, gf_ref, y_hbm, o_ref, bufa, bufb, sem,
                    *, tc, final_norm):
    i = pl.program_id(0)
    slot = i % 2

    def issue(par, pbr, s):
        def body(t, c):
            pltpu.make_async_copy(y_hbm.at[pl.ds(par[t], 1), :], bufa.at[s, pl.ds(t, 1), :], sem.at[s]).start()
            pltpu.make_async_copy(y_hbm.at[pl.ds(pbr[t], 1), :], bufb.at[s, pl.ds(t, 1), :], sem.at[s]).start()
            return c
        lax.fori_loop(0, tc, body, 0)

    @pl.when(i == 0)
    def _():
        issue(pa_ref, pb_ref, 0)

    @pl.when(i + 1 < pl.num_programs(0))
    def _():
        issue(pan_ref, pbn_ref, 1 - slot)

    pltpu.make_async_copy(y_hbm.at[pl.ds(0, tc), :], bufa.at[slot], sem.at[slot]).wait()
    pltpu.make_async_copy(y_hbm.at[pl.ds(0, tc), :], bufb.at[slot], sem.at[slot]).wait()
    wts = wts_ref[...]
    out = x_ref[...] + wts[:, 0:1] * bufa[slot] + wts[:, 1:2] * bufb[slot]
    if final_norm:
        out = _rmsnorm(out, gf_ref[...])
    o_ref[...] = out


def _combine(x, y, pa, pb, wts, g_final, final_norm):
    m = x.shape[0]
    tc = min(m, 256)
    n = m // tc
    cur = pl.BlockSpec((tc,), lambda i: (i,), memory_space=pltpu.SMEM)
    nxt = pl.BlockSpec((tc,), lambda i: (jnp.minimum(i + 1, n - 1),), memory_space=pltpu.SMEM)
    rows = pl.BlockSpec((tc, D_MODEL), lambda i: (i, 0))
    return pl.pallas_call(
        functools.partial(_combine_kernel, tc=tc, final_norm=final_norm),
        grid=(n,),
        in_specs=[cur, cur, nxt, nxt, rows, pl.BlockSpec((tc, LANE), lambda i: (i, 0)), _const_spec(g_final),
                  pl.BlockSpec(memory_space=pl.ANY)],
        out_specs=rows,
        out_shape=jax.ShapeDtypeStruct((m, D_MODEL), F32),
        scratch_shapes=[pltpu.VMEM((2, tc, D_MODEL), F32), pltpu.VMEM((2, tc, D_MODEL), F32),
                        pltpu.SemaphoreType.DMA((2,))],
        compiler_params=_cparams(("arbitrary",)),
        name="moe_combine",
    )(pa, pb, pa, pb, x, wts, g_final, y)


def _moe_routed(x, g, wr_pad, wg, wu, wd, g_final, final_norm):
    m = x.shape[0]
    tm = min(MOE_TM, m)
    comb, sel, cnt = _router(x, g, wr_pad)
    counts = cnt[0, :N_EXPERTS].astype(jnp.int32)
    tiles = (counts + tm - 1) // tm
    ends = jnp.cumsum(tiles)
    off = jnp.zeros((1, LANE), F32).at[0, :N_EXPERTS].set(((ends - tiles) * tm).astype(F32))
    n_tiles = (2 * m) // tm + N_EXPERTS
    tile_expert = jnp.minimum(jnp.searchsorted(ends, jnp.arange(n_tiles, dtype=jnp.int32), side="right"),
                              N_EXPERTS - 1).astype(jnp.int32)
    n_used = ends[-1:].astype(jnp.int32)
    pa, pb, wts = _positions(sel, comb, off)
    xs = _dispatch(x, pa, pb, n_tiles * tm)
    y = _experts(xs, g, tile_expert, n_used, wg, wu, wd, tm)
    return _combine(x, y, pa, pb, wts, g_final, final_norm)


def _moe_dense_kernel(x_ref, g_ref, comb_ref, wg_ref, wu_ref, wd_ref, o_ref, h_scr):
    e = pl.program_id(1)
    f = pl.program_id(2)

    @pl.when((e == 0) & (f == 0))
    def _():
        x = x_ref[...]
        h_scr[...] = _rmsnorm(x, g_ref[...]).astype(BF16)
        o_ref[...] = x

    h = h_scr[...]
    comb = comb_ref[...]
    lane = lax.broadcasted_iota(jnp.int32, comb.shape, 1)
    ce = jnp.sum(jnp.where(lane == e, comb, 0.0), axis=-1, keepdims=True)
    act = (_silu(_dot(h, wg_ref[0])) * _dot(h, wu_ref[0])).astype(BF16)
    o_ref[...] += ce * _dot(act, wd_ref[0])


def _moe_dense(x, g, comb, wg, wu, wd):
    m = x.shape[0]
    d_exp = wg.shape[2]
    tm = min(m, 512)
    tf = d_exp // 4
    rows = pl.BlockSpec((tm, D_MODEL), lambda i, e, f: (i, 0))
    return pl.pallas_call(
        _moe_dense_kernel,
        grid=(m // tm, N_EXPERTS, d_exp // tf),
        in_specs=[rows, pl.BlockSpec((1, D_MODEL), lambda i, e, f: (0, 0)),
                  pl.BlockSpec((tm, LANE), lambda i, e, f: (i, 0)),
                  pl.BlockSpec((1, D_MODEL, tf), lambda i, e, f: (e, 0, f)),
                  pl.BlockSpec((1, D_MODEL, tf), lambda i, e, f: (e, 0, f)),
                  pl.BlockSpec((1, tf, D_MODEL), lambda i, e, f: (e, f, 0))],
        out_specs=rows,
        out_shape=jax.ShapeDtypeStruct((m, D_MODEL), F32),
        scratch_shapes=[pltpu.VMEM((tm, D_MODEL), BF16)],
        compiler_params=_cparams(("parallel", "arbitrary", "arbitrary")),
        name="moe_dense",
    )(x, g, comb, wg, wu, wd)


def _final_norm_kernel(x_ref, g_ref, o_ref):
    o_ref[...] = _rmsnorm(x_ref[...], g_ref[...])


def _final_norm(x, g):
    m = x.shape[0]
    tm = min(m, 1024)
    rows = pl.BlockSpec((tm, D_MODEL), lambda i: (i, 0))
    return pl.pallas_call(
        _final_norm_kernel,
        grid=(m // tm,),
        in_specs=[rows, _const_spec(g)],
        out_specs=rows,
        out_shape=jax.ShapeDtypeStruct((m, D_MODEL), F32),
        compiler_params=_cparams(("parallel",)),
        name="final_norm",
    )(x, g)


def _prep_w_in(w):
    main = w[:, :Z_MAIN]
    ff = jnp.pad(w[:, Z_MAIN:Z_MAIN + H], ((0, 0), (0, LANE - H)))
    gates = w[:, Z_MAIN + H:]
    return jnp.concatenate([main, gates], axis=1).astype(BF16), ff.astype(BF16)


def _rope_tables(pos):
    half = HD // 2
    inv = ROPE_BASE ** (-jnp.arange(half, dtype=F32) / half)
    ang = pos.astype(F32)[:, None] * inv[None, :]
    cos = jnp.cos(ang)
    sin = jnp.sin(ang)
    cos512 = jnp.tile(jnp.concatenate([cos, cos], axis=1), (1, H))
    sin512 = jnp.tile(jnp.concatenate([-sin, sin], axis=1), (1, H))
    return cos512, sin512


def kernel(x_prompt, x_sample, state_ret, state_conv, cache_k, cache_v, cache_logf, page_table, g_mix, w_in,
           b_forget, conv_w, w_ret_o, w_conv_o, w_fox_o, w_o, g_ffn, w_ffn_gate, w_ffn_up, w_ffn_down, w_router,
           w_exp_gate, w_exp_up, w_exp_down, g_final):
    b, l, _ = x_prompt.shape
    bd, t, _ = x_sample.shape
    assert t == 1, "decode group carries one new position per sequence"
    depth = w_in.shape[0]
    n_pages = page_table.shape[1]
    n_phys = cache_k.shape[1]

    log_gamma = jnp.log(1.0 - jnp.exp2(-5.0 - jnp.arange(H, dtype=F32)))
    tables = _ret_tables(log_gamma)
    gamma = jnp.exp(log_gamma)
    cos_p, sin_p = _rope_tables(jnp.arange(l))
    cos_s, sin_s = _rope_tables(n_pages * PAGE + jnp.arange(1))
    cache_kt = jnp.transpose(cache_k, (0, 1, 3, 4, 2))
    cache_vt = jnp.transpose(cache_v, (0, 1, 3, 4, 2))
    cache_lft = jnp.transpose(cache_logf, (0, 1, 3, 2))
    g_fin = g_final.reshape(1, D_MODEL)
    z_layers = []

    xp = x_prompt.reshape(b * l, D_MODEL)
    xs = x_sample.reshape(bd, D_MODEL)
    outs = {k: [] for k in ("ret_p", "ret_s", "buf_p", "buf_s", "lfp", "ks", "vs", "lfs")}
    for layer in range(depth):
        last = layer == depth - 1
        w_in_l, w_ff_l = _prep_w_in(w_in[layer])
        g_l = g_mix[layer].reshape(1, D_MODEL)
        b_pad = jnp.pad(b_forget[layer], (0, LANE - H)).reshape(1, LANE)
        wr_o = w_ret_o[layer].astype(BF16)
        wc_o = w_conv_o[layer].astype(BF16)
        wf_o = w_fox_o[layer].astype(BF16)
        wo = w_o[layer].astype(BF16)
        cw = conv_w[layer]

        z, zb, ff = _inproj(xp, g_l, w_in_l, w_ff_l)
        o_ret, st = _ret_prompt(z, cos_p, sin_p, tables, b, l)
        lf, bias = _forget_prompt(ff, b_pad, b, l)
        o_fox = _fox_prompt(zb, bias.reshape(b, H // 2, 2, l), b, l)
        xp, buf = _merge_prompt(xp, z, o_ret, o_fox, cw, wr_o, wc_o, wf_o, wo, b, l)
        outs["ret_p"].append(st)
        outs["buf_p"].append(buf)
        z_layers.append(z)
        outs["lfp"].append(lf.reshape(b, l, H))

        zs, _, ffs = _inproj(xs, g_l, w_in_l, w_ff_l)
        seg = lambda j: zs[:, j * D_BR:(j + 1) * D_BR]
        o_ret_s, st_s = _ret_step(zs, state_ret[layer], cos_s, sin_s, gamma)
        lf_s = _forget_step(ffs, b_pad)[:, :H]
        o_fox_s = _fox_decode(page_table, seg(_FQ).reshape(bd, H, HD, 1), seg(_FK).reshape(bd, H, HD, 1),
                              seg(_FV).reshape(bd, H, HD, 1), lf_s.reshape(bd, H, 1),
                              cache_kt, cache_vt, cache_lft, layer).reshape(bd, D_BR)
        xs, u_s = _merge_step(xs, zs, o_ret_s, o_fox_s, state_conv[layer, :, 0], state_conv[layer, :, 1],
                              cw, wr_o, wc_o, wf_o, wo)
        outs["ret_s"].append(st_s)
        outs["buf_s"].append(jnp.stack([state_conv[layer, :, 1], u_s], axis=1))
        outs["ks"].append(seg(_FK).reshape(bd, 1, H, HD))
        outs["vs"].append(seg(_FV).reshape(bd, 1, H, HD))
        outs["lfs"].append(lf_s.reshape(bd, 1, H))

        g_f = g_ffn[layer].reshape(1, D_MODEL)
        mi = layer // 2
        if layer % 2 == 0:
            wg = w_ffn_gate[mi].astype(BF16)
            wu = w_ffn_up[mi].astype(BF16)
            wd = w_ffn_down[mi].astype(BF16)
            xp = _ffn(xp, g_f, wg, wu, wd)
            xs = _ffn(xs, g_f, wg, wu, wd)
        else:
            wr_pad = jnp.pad(w_router[mi], ((0, 0), (0, LANE - N_EXPERTS)))
            wg = w_exp_gate[mi].astype(BF16)
            wu = w_exp_up[mi].astype(BF16)
            wd = w_exp_down[mi].astype(BF16)
            xp = _moe_routed(xp, g_f, wr_pad, wg, wu, wd, g_fin, last)
            xs = _moe_dense(xs, g_f, _router(xs, g_f, wr_pad)[0], wg, wu, wd)
        if last and layer % 2 == 0:
            xp = _final_norm(xp, g_fin)

    y_prompt = xp.reshape(b, l, D_MODEL)
    y_sample = _final_norm(xs, g_fin).reshape(bd, 1, D_MODEL)
    st = lambda k: jnp.stack(outs[k])
    k_prompt, v_prompt = (jnp.transpose(a.reshape(depth, b, H, HD, l), (0, 1, 4, 2, 3))
                          for a in _kv_transposed(z_layers, b, l))
    return (y_prompt, y_sample, st("ret_p"), st("ret_s"), st("buf_p"), st("buf_s"),
            k_prompt, v_prompt, st("lfp"), st("ks"), st("vs"), st("lfs"))
```

```python
import functools

import jax
import jax.numpy as jnp
import numpy as np
from jax import lax
from jax.experimental import pallas as pl
from jax.experimental.pallas import tpu as pltpu

F32 = jnp.float32
BF16 = jnp.bfloat16

D_MODEL = 1024
H = 8
HD = 64
D_BR = H * HD
RET_CHUNK = 128
PAGE = 128
DEC_PAGES = 16
N_EXPERTS = 8
ROPE_BASE = 10000.0
EPS = 1e-6
LOG2E = 1.4426950408889634
LANE = 128
VMEM_LIMIT = 56 * 1024 * 1024

_RQ, _RK, _RV, _RG, _CB, _CC, _CX, _FQ, _FK, _FV = range(10)
Z_MAIN = 10 * D_BR
Z_COLS = Z_MAIN + 3 * D_MODEL
Z_TN = 2048
FOX_TQ = 1024
FOX_TK = 1024
FOX_TD = 512
RET_ROWS = 512
MOE_TM = 1024
MOE_NF = 7
DMA_UNROLL = 4


def _cparams(sem):
    return pltpu.CompilerParams(dimension_semantics=sem, vmem_limit_bytes=VMEM_LIMIT)


def _rmsnorm(x, g):
    return x * lax.rsqrt(jnp.mean(x * x, axis=-1, keepdims=True) + EPS) * g


def _dot(a, b):
    return jnp.dot(a, b, preferred_element_type=F32)


def _dot_nt(a, b):
    return lax.dot_general(a, b, (((1,), (1,)), ((), ())), preferred_element_type=F32)


def _dot_tn(a, b):
    return lax.dot_general(a, b, (((0,), (0,)), ((), ())), preferred_element_type=F32)


def _dot_f32(a, b):
    return jnp.dot(a, b, preferred_element_type=F32, precision=lax.Precision.HIGHEST)


def _dot_nt_f32(a, b):
    return lax.dot_general(a, b, (((1,), (1,)), ((), ())), preferred_element_type=F32,
                           precision=lax.Precision.HIGHEST)


def _sigmoid(x):
    return 1.0 / (1.0 + jnp.exp(-x))


def _silu(x):
    return x * _sigmoid(x)


def _log_sigmoid(x):
    return jnp.minimum(x, 0.0) - jnp.log(1.0 + jnp.exp(-jnp.abs(x)))


def _const_spec(a):
    n = a.ndim
    return pl.BlockSpec(a.shape, lambda *_, n=n: (0,) * n)


def _inproj_kernel(x_ref, g_ref, w_ref, wf_ref, z_ref, zb_ref, ff_ref, h_scr):
    j = pl.program_id(1)

    @pl.when(j == 0)
    def _():
        h = _rmsnorm(x_ref[...], g_ref[...]).astype(BF16)
        h_scr[...] = h
        ff_ref[...] = _dot(h, wf_ref[...])

    z = _dot(h_scr[...], w_ref[...])
    z_ref[...] = z

    @pl.when(j == 1)
    def _():
        zb_ref[:, 0:D_BR] = (z[:, 3 * D_BR:] * (HD ** -0.5 * LOG2E)).astype(BF16)

    @pl.when(j == 2)
    def _():
        zb_ref[:, D_BR:] = z[:, :2 * D_BR].astype(BF16)


def _inproj(x, g, w, w_ff):
    m = x.shape[0]
    tm = min(m, 1024)
    assert Z_COLS // Z_TN == 4 and _FQ * D_BR == Z_TN + 3 * D_BR and _FK * D_BR == 2 * Z_TN
    return pl.pallas_call(
        _inproj_kernel,
        grid=(m // tm, Z_COLS // Z_TN),
        in_specs=[
            pl.BlockSpec((tm, D_MODEL), lambda i, j: (i, 0)),
            pl.BlockSpec((1, D_MODEL), lambda i, j: (0, 0)),
            pl.BlockSpec((D_MODEL, Z_TN), lambda i, j: (0, j)),
            pl.BlockSpec((D_MODEL, LANE), lambda i, j: (0, 0)),
        ],
        out_specs=[pl.BlockSpec((tm, Z_TN), lambda i, j: (i, j)),
                   pl.BlockSpec((tm, 3 * D_BR), lambda i, j: (i, 0)),
                   pl.BlockSpec((tm, LANE), lambda i, j: (i, 0))],
        out_shape=[jax.ShapeDtypeStruct((m, Z_COLS), F32),
                   jax.ShapeDtypeStruct((m, 3 * D_BR), BF16),
                   jax.ShapeDtypeStruct((m, LANE), F32)],
        scratch_shapes=[pltpu.VMEM((tm, D_MODEL), BF16)],
        compiler_params=_cparams(("parallel", "arbitrary")),
        name="inproj",
    )(x, g, w, w_ff)


def _rope(x, cos, sin_signed):
    lane = lax.broadcasted_iota(jnp.int32, x.shape, 1)
    first = (lane % HD) < (HD // 2)
    n = x.shape[1]
    partner = jnp.where(first, pltpu.roll(x, n - HD // 2, 1), pltpu.roll(x, HD // 2, 1))
    return x * cos + partner * sin_signed


def _ret_prompt_kernel(rq_ref, rk_ref, rv_ref, rg_ref, cos_ref, sin_ref, dec_ref, qd_ref, kd_ref, sd_ref,
                       o_ref, st_ref, s_scr, *, chunks):
    c = pl.program_id(1)

    @pl.when(c == 0)
    def _():
        s_scr[...] = jnp.zeros_like(s_scr)

    low = lax.broadcasted_iota(jnp.int32, (RET_CHUNK, LANE), 1) < HD
    row = lax.broadcasted_iota(jnp.int32, (LANE, LANE), 0)
    col = lax.broadcasted_iota(jnp.int32, (LANE, LANE), 1)
    same_head = (row < HD) == (col < HD)
    states = [s_scr[p] for p in range(H // 2)]
    for ch in range(chunks):
        rs = slice(ch * RET_CHUNK, (ch + 1) * RET_CHUNK)
        cos = cos_ref[rs, :]
        sin = sin_ref[rs, :]
        q = _rope(rq_ref[rs, :], cos, sin)
        k = _rope(rk_ref[rs, :], cos, sin) * (HD ** -0.5)
        v = rv_ref[rs, :]
        for p in range(H // 2):
            sl = slice(p * LANE, (p + 1) * LANE)
            qp = q[:, sl]
            kt = k[:, sl].T
            ktb = kt.astype(BF16)
            vp = v[:, sl].astype(BF16)
            inner = []
            for hh in range(2):
                qm = jnp.where(low if hh == 0 else jnp.logical_not(low), qp, 0.0).astype(BF16)
                sc = _dot(qm, ktb) * dec_ref[2 * p + hh]
                inner.append(_dot(sc.astype(BF16), vp))
            s_old = states[p]
            cross = _dot(qp.astype(BF16), s_old.astype(BF16)) * qd_ref[:, sl]
            o = jnp.where(low, inner[0], inner[1]) + cross
            kdec = (kt * kd_ref[sl, :]).astype(BF16)
            states[p] = sd_ref[p] * s_old + jnp.where(same_head, _dot(kdec, vp), 0.0)
            oo = o * o
            ms = jnp.where(low, jnp.sum(jnp.where(low, oo, 0.0), axis=-1, keepdims=True),
                           jnp.sum(jnp.where(low, 0.0, oo), axis=-1, keepdims=True)) * (1.0 / HD)
            o_ref[rs, sl] = (o * lax.rsqrt(ms + EPS) * _silu(rg_ref[rs, sl])).astype(BF16)
    for p in range(H // 2):
        s_scr[p] = states[p]

    @pl.when(c == pl.num_programs(1) - 1)
    def _():
        for p in range(H // 2):
            st_ref[0, 2 * p] = states[p][:HD, :HD]
            st_ref[0, 2 * p + 1] = states[p][HD:, HD:]


def _ret_tables(log_gamma):
    i = jnp.arange(RET_CHUNK, dtype=F32)
    diff = i[:, None] - i[None, :]
    decay = jnp.where(diff >= 0, jnp.exp(log_gamma[:, None, None] * jnp.maximum(diff, 0.0)), 0.0)
    q_decay = jnp.exp(log_gamma[:, None] * (i + 1.0)[None, :]).T
    k_decay = jnp.exp(log_gamma[:, None] * (RET_CHUNK - 1.0 - i)[None, :]).T
    s_decay = jnp.exp(log_gamma * RET_CHUNK)
    qd = jnp.repeat(q_decay, HD, axis=1)
    kd = jnp.repeat(k_decay, HD, axis=1).T
    sd =jnp.broadcast_to(jnp.repeat(s_decay, HD).reshape(H // 2, LANE, 1), (H // 2, LANE, LANE))
    return decay, qd, kd, sd


def _ret_prompt(z, cos, sin, tables, b, l):
    decay, qd, kd, sd = tables
    tr = min(l, RET_ROWS)
    nc = l // tr
    blk = lambda j: pl.BlockSpec((tr, D_BR), lambda bi, c, j=j: (bi * nc + c, j))
    tab = pl.BlockSpec((tr, D_BR), lambda bi, c: (c, 0))
    return pl.pallas_call(
        functools.partial(_ret_prompt_kernel, chunks=tr // RET_CHUNK),
        grid=(b, nc),
        in_specs=[blk(_RQ), blk(_RK), blk(_RV), blk(_RG), tab, tab,
                  _const_spec(decay), _const_spec(qd), _const_spec(kd), _const_spec(sd)],
        out_specs=[pl.BlockSpec((tr, D_BR), lambda bi, c: (bi * nc + c, 0)),
                   pl.BlockSpec((1, H, HD, HD), lambda bi, c: (bi, 0, 0, 0))],
        out_shape=[jax.ShapeDtypeStruct((b * l, D_BR), BF16),
                   jax.ShapeDtypeStruct((b, H, HD, HD), F32)],
        scratch_shapes=[pltpu.VMEM((H // 2, LANE, LANE), F32)],
        compiler_params=_cparams(("parallel", "arbitrary")),
        name="ret_prompt",
    )(z, z, z, z, cos, sin, decay, qd, kd, sd)


def _ret_step_kernel(q_ref, k_ref, v_ref, g_ref, s_ref, cos_ref, sin_ref, gam_ref, grow_ref, o_ref, sn_ref):
    cos = cos_ref[...]
    sin = sin_ref[...]
    q = _rope(q_ref[0], cos, sin).astype(BF16)
    k = (_rope(k_ref[0], cos, sin) * (HD ** -0.5)).astype(BF16)
    own = (lax.broadcasted_iota(jnp.int32, (H, D_BR), 1) // HD) == lax.broadcasted_iota(jnp.int32, (H, D_BR), 0)
    q_bd = jnp.where(own, jnp.broadcast_to(q.astype(F32), (H, D_BR)), 0.0)
    k_bd = jnp.where(own, jnp.broadcast_to(k.astype(F32), (H, D_BR)), 0.0)
    v = v_ref[0].astype(BF16)
    sc = jnp.sum(q_bd * k.astype(F32), axis=-1, keepdims=True)
    inner = sc.astype(BF16).astype(F32) * v.astype(F32)
    s_old = s_ref[0]
    cross = _dot(q_bd.astype(BF16), s_old.astype(BF16)) * gam_ref[...]
    o = inner + cross
    o = o * lax.rsqrt(jnp.mean(o * o, axis=-1, keepdims=True) + EPS)
    o_ref[0] = o * _silu(g_ref[0])
    sn_ref[0] = grow_ref[...] * s_old + _dot_tn(k_bd.astype(BF16), v)


def _ret_step(zs, state, cos512, sin512, gamma):
    bd = zs.shape[0]
    seg = lambda j: zs[:, j * D_BR:(j + 1) * D_BR]
    s2 = state.reshape(bd, D_BR, HD)
    gam = gamma.reshape(H, 1)
    grow = jnp.broadcast_to(jnp.repeat(gamma, HD)[:, None], (D_BR, HD))
    flat = pl.BlockSpec((1, 1, D_BR), lambda i: (i, 0, 0))
    r3 = pl.BlockSpec((1, H, HD), lambda i: (i, 0, 0))
    s3 = pl.BlockSpec((1, D_BR, HD), lambda i: (i, 0, 0))
    o, sn = pl.pallas_call(
        _ret_step_kernel,
        grid=(bd,),
        in_specs=[flat, flat, r3, r3, s3,
                  _const_spec(cos512), _const_spec(sin512), _const_spec(gam), _const_spec(grow)],
        out_specs=[r3, s3],
        out_shape=[jax.ShapeDtypeStruct((bd, H, HD), F32),
                   jax.ShapeDtypeStruct((bd, D_BR, HD), F32)],
        compiler_params=_cparams(("parallel",)),
        name="ret_step",
    )(seg(_RQ).reshape(bd, 1, D_BR), seg(_RK).reshape(bd, 1, D_BR), seg(_RV).reshape(bd, H, HD),
      seg(_RG).reshape(bd, H, HD), s2, cos512, sin512, gam, grow)
    return o.reshape(bd, D_BR), sn.reshape(bd, H, HD, HD)


def _forget_prompt_kernel(ff_ref, b_ref, tri_ref, lf_ref, bias_ref, carry):
    @pl.when(pl.program_id(1) == 0)
    def _():
        carry[...] = jnp.zeros_like(carry)

    lf = _log_sigmoid(ff_ref[...] + b_ref[...])
    lf_ref[...] = lf[:, :H]
    c = _dot_f32(tri_ref[...], lf) + carry[...]
    bias_ref[0] = c.T[:H, :] * (-LOG2E)
    carry[...] = c[-1:, :]


def _forget_prompt(ff, b_pad, b, l):
    tc = min(l, 512)
    nt = l // tc
    tri = jnp.asarray(np.tril(np.ones((tc, tc), np.float32)))
    return pl.pallas_call(
        _forget_prompt_kernel,
        grid=(b, nt),
        in_specs=[pl.BlockSpec((tc, LANE), lambda bi, t: (bi * nt + t, 0)),
                  _const_spec(b_pad), _const_spec(tri)],
        out_specs=[pl.BlockSpec((tc, H), lambda bi, t: (bi * nt + t, 0)),
                   pl.BlockSpec((1, H, tc), lambda bi, t: (bi, 0, t))],
        out_shape=[jax.ShapeDtypeStruct((b * l, H), F32),
                   jax.ShapeDtypeStruct((b, H, l), F32)],
        scratch_shapes=[pltpu.VMEM((1, LANE), F32)],
        compiler_params=_cparams(("parallel", "arbitrary")),
        name="forget_prompt",
    )(ff, b_pad, tri)


def _forget_step_kernel(ff_ref, b_ref, lf_ref):
    lf_ref[...] = _log_sigmoid(ff_ref[...] + b_ref[...])


def _forget_step(ff, b_pad):
    m = ff.shape[0]
    full = pl.BlockSpec((m, LANE), lambda i: (0, 0))
    return pl.pallas_call(
        _forget_step_kernel,
        grid=(1,),
        in_specs=[full, _const_spec(b_pad)],
        out_specs=full,
        out_shape=jax.ShapeDtypeStruct((m, LANE), F32),
        name="forget_step",
    )(ff, b_pad)


def _fox_prompt_kernel(q_ref, k_ref, v_ref, bias_ref, o_ref, *, tq, tk, td):
    qi = pl.program_id(2)
    zero = jnp.zeros((), BF16)
    one = jnp.ones((), BF16)
    lowq = lax.broadcasted_iota(jnp.int32, (tq, LANE), 1) < HD
    q = q_ref[...]
    qm = (jnp.where(lowq, q, zero), jnp.where(lowq, zero, q))

    def step(ks, width, r0, carry, masked):
        rows = tq - r0
        lowk = lax.broadcasted_iota(jnp.int32, (width, LANE), 1) < HD
        kb = k_ref[pl.ds(ks, width), :]
        vb = v_ref[pl.ds(ks, width), :]
        vh = (jnp.where(lowk, vb, one), jnp.where(lowk, one, vb))
        new = []
        for hh in range(2):
            m_all, acc_all = carry[hh]
            m, acc = m_all[r0:], acc_all[r0:]
            s = _dot_nt(qm[hh][r0:], kb) + bias_ref[0, 0, pl.ds(hh, 1), pl.ds(ks, width)]
            if masked:
                qpos = qi * tq + r0 + lax.broadcasted_iota(jnp.int32, (rows, width), 0)
                kpos = ks + lax.broadcasted_iota(jnp.int32, (rows, width), 1)
                s = jnp.where(kpos <= qpos, s, -jnp.inf)
            m_new = jnp.maximum(m, jnp.max(s, axis=-1, keepdims=True))
            p = jnp.exp2(s - m_new)
            acc = jnp.exp2(m - m_new) * acc + _dot(p.astype(BF16), vh[hh])
            if r0:
                m_new = jnp.concatenate([m_all[:r0], m_new], axis=0)
                acc = jnp.concatenate([acc_all[:r0], acc], axis=0)
            new.append((m_new, acc))
        return tuple(new)

    init = tuple((jnp.full((tq, 1), -jnp.inf, F32), jnp.zeros((tq, LANE), F32)) for _ in range(2))
    n_full = qi * (tq // tk)
    carry = lax.fori_loop(0, n_full, lambda j, c: step(pl.multiple_of(j * tk, tk), tk, 0, c, False), init)
    for d in range(tq // td):
        carry = step(pl.multiple_of(qi * tq + d * td, td), td, d * td, carry, True)
    (_, acc0), (_, acc1) = carry
    out0 = acc0 / acc0[:, HD:HD + 1]
    out1 = acc1 / acc1[:, 0:1]
    o_ref[...] = jnp.where(lowq, out0, out1).astype(BF16)


def _fox_prompt(zb, bias, b, l):
    tq = min(l, FOX_TQ)
    tk = min(tq, FOX_TK)
    td = min(tk, FOX_TD)
    nq = l // tq
    npair = H // 2
    qspec = pl.BlockSpec((tq, LANE), lambda bi, p, qi: (bi * nq + qi, p))
    kspec = pl.BlockSpec((l, LANE), lambda bi, p, qi: (bi, npair + p))
    vspec = pl.BlockSpec((l, LANE), lambda bi, p, qi: (bi, 2 * npair + p))
    return pl.pallas_call(
        functools.partial(_fox_prompt_kernel, tq=tq, tk=tk, td=td),
        grid=(b, npair, nq),
        in_specs=[qspec, kspec, vspec,
                  pl.BlockSpec((1, 1, 2, l), lambda bi, p, qi: (bi, p, 0, 0))],
        out_specs=qspec,
        out_shape=jax.ShapeDtypeStruct((b * l, D_BR), BF16),
        compiler_params=_cparams(("parallel", "parallel", "arbitrary")),
        name="fox_prompt",
    )(zb, zb, zb, bias)


def _fox_decode_kernel(pt_ref, q_ref, kn_ref, vn_ref, lfn_ref, suf_ref, *rest, g):
    del pt_ref
    k_refs = rest[:g]
    v_refs = rest[g:2 * g]
    lf_refs = rest[2 * g:3 * g]
    o_ref = rest[3 * g]
    m_scr, l_scr, acc_scr, cy_scr = rest[3 * g + 1:]
    j = pl.program_id(1)
    lane = lax.broadcasted_iota(jnp.int32, (HD, PAGE), 1)
    qcol = [q_ref[0, h] * (HD ** -0.5 * LOG2E) for h in range(H)]

    @pl.when(j == 0)
    def _():
        m_scr[...] = jnp.concatenate(
            [jnp.sum(qcol[h] * kn_ref[0, h], axis=0, keepdims=True) for h in range(H)], axis=0)
        l_scr[...] = jnp.ones_like(l_scr)
        for h in range(H):
            acc_scr[h] = jnp.where(lane == 0, vn_ref[0, h], 0.0)
        cy_scr[...] = lfn_ref[0] * LOG2E

    lf = jnp.concatenate([lf_refs[t][0, 0] for t in range(g)], axis=0) * LOG2E
    suf = _dot_f32(lf, suf_ref[...])
    tot = jnp.sum(lf, axis=1, keepdims=True)
    cy = cy_scr[...]
    rows = []
    for t in range(g):
        s_t = jnp.concatenate(
            [jnp.sum(k_refs[t][0, 0, h] * qcol[h], axis=0, keepdims=True) for h in range(H)], axis=0)
        rows.append(s_t + suf[t * H:(t + 1) * H] + cy)
        cy = cy + tot[t * H:(t + 1) * H]
    cy_scr[...] = cy
    m_old = m_scr[...]
    m_new = m_old
    for t in range(g):
        m_new = jnp.maximum(m_new, jnp.max(rows[t], axis=1, keepdims=True))
    alpha = jnp.exp2(m_old - m_new)
    ps = [jnp.exp2(rows[t] - m_new) for t in range(g)]
    l_new = alpha * l_scr[...]
    for t in range(g):
        l_new = l_new + jnp.sum(ps[t], axis=1, keepdims=True)
    l_scr[...] = l_new
    m_scr[...] = m_new
    for h in range(H):
        acc = alpha[h:h + 1, :] * acc_scr[h]
        for t in range(g):
            acc = acc + v_refs[t][0, 0, h] * ps[t][h:h + 1, :]
        acc_scr[h] = acc

    @pl.when(j == pl.num_programs(1) - 1)
    def _():
        l = l_scr[...]
        for h in range(H):
            o_ref[0, h] = jnp.sum(acc_scr[h], axis=1, keepdims=True) / l[h:h + 1, :]


def _fox_decode(page_table, qcol, kncol, vncol, lfn, cache_kt, cache_vt, cache_lft, layer):
    bd, n_pages = page_table.shape
    g = DEC_PAGES if n_pages % DEC_PAGES == 0 else 1
    steps = n_pages // g
    jj = np.arange(PAGE)
    suf = jnp.asarray((jj[:, None] > jj[None, :]).astype(np.float32))

    def page(nd):
        def spec(t):
            return lambda b, j, pt, t=t: (layer, pt[b, n_pages - 1 - (j * g + t)]) + (0,) * nd
        return spec

    c4 = pl.BlockSpec((1, H, HD, 1), lambda b, j, pt: (b, 0, 0, 0))
    in_specs = [c4, c4, c4, pl.BlockSpec((1, H, 1), lambda b, j, pt: (b, 0, 0)),
                pl.BlockSpec((PAGE, PAGE), lambda b, j, pt: (0, 0))]
    in_specs += [pl.BlockSpec((1, 1, H, HD, PAGE), page(3)(t)) for t in range(g)]
    in_specs += [pl.BlockSpec((1, 1, H, HD, PAGE), page(3)(t)) for t in range(g)]
    in_specs += [pl.BlockSpec((1, 1, H, PAGE), page(2)(t)) for t in range(g)]
    grid_spec = pltpu.PrefetchScalarGridSpec(
        num_scalar_prefetch=1,
        grid=(bd, steps),
        in_specs=in_specs,
        out_specs=c4,
        scratch_shapes=[pltpu.VMEM((H, 1), F32), pltpu.VMEM((H, 1), F32),
                        pltpu.VMEM((H, HD, PAGE), F32), pltpu.VMEM((H, 1), F32)],
    )
    return pl.pallas_call(
        functools.partial(_fox_decode_kernel, g=g),
        grid_spec=grid_spec,
        out_shape=jax.ShapeDtypeStruct((bd, H, HD, 1), F32),
        compiler_params=_cparams(("parallel", "arbitrary")),
        name="fox_decode",
    )(page_table, qcol, kncol, vncol, lfn, suf, *([cache_kt] * g), *([cache_vt] * g), *([cache_lft] * g))


def _kv_transposed_kernel(*refs):
    depth = len(refs) // 2 - 1
    kt_ref, vt_ref = refs[-2], refs[-1]
    for layer in range(depth):
        kt_ref[layer, 0] = refs[2 * layer][...].T
        vt_ref[layer, 0] = refs[2 * layer + 1][...].T


def _kv_transposed(zs, b, l):
    depth = len(zs)
    m = zs[0].shape[0]
    tm = min(l, 1024)
    tps = l // tm
    zspec = lambda j: pl.BlockSpec((tm, D_BR), lambda i, j=j: (i, j))
    ospec = pl.BlockSpec((depth, 1, D_BR, tm), lambda i: (0, i // tps, 0, i % tps))
    shape = jax.ShapeDtypeStruct((depth, b, D_BR, l), F32)
    return pl.pallas_call(
        _kv_transposed_kernel,
        grid=(m // tm,),
        in_specs=[zspec(_FK), zspec(_FV)] * depth,
        out_specs=[ospec, ospec],
        out_shape=[shape, shape],
        compiler_params=_cparams(("parallel",)),
        name="kv_transposed",
    )(*[z for z in zs for _ in range(2)])


def _merge_math(x, o_ret, y_conv, o_fox, a1, a2, a3, wr_ref, wc_ref, wf_ref, wo_ref):
    merged = (_sigmoid(a1) * _dot(o_ret, wr_ref[...])
              + _sigmoid(a2) * _dot(y_conv, wc_ref[...])
              + _sigmoid(a3) * _dot(o_fox, wf_ref[...]))
    return x + _dot(merged.astype(BF16), wo_ref[...])


def _merge_prompt_kernel(x_ref, oret_ref, ofox_ref, cb_ref, cc_ref, cx_ref, ccp_ref, cxp_ref,
                         a1_ref, a2_ref, a3_ref, cw_ref, wr_ref, wc_ref, wf_ref, wo_ref,
                         xo_ref, buf_ref, *, tm, tiles_per_seq):
    i = pl.program_id(0)
    u = cc_ref[...] * cx_ref[...]
    prev = ccp_ref[...] * cxp_ref[...]
    prev = jnp.where(i % tiles_per_seq == 0, 0.0, prev)
    p1 = prev[7:8, :]
    p2 = prev[6:7, :]
    row = lax.broadcasted_iota(jnp.int32, u.shape, 0)
    u1 = jnp.where(row >= 1, pltpu.roll(u, 1, 0), p1)
    u2 = jnp.where(row >= 2, pltpu.roll(u, 2, 0), jnp.where(row == 1, p1, p2))
    y = u2 * cw_ref[0:1, :] + u1 * cw_ref[1:2, :] + u * cw_ref[2:3, :]
    y_conv = (cb_ref[...] * y).astype(BF16)
    buf_ref[0] = u[tm - 2:, :]
    xo_ref[...] = _merge_math(x_ref[...], oret_ref[...], y_conv, ofox_ref[...],
                              a1_ref[...], a2_ref[...], a3_ref[...], wr_ref, wc_ref, wf_ref, wo_ref)


def _merge_prompt(x, z, o_ret, o_fox, conv_w, w_ret_o, w_conv_o, w_fox_o, w_o, b, l):
    m = b * l
    tm = min(l, 512)
    tps = l // tm
    zb = lambda j: pl.BlockSpec((tm, D_BR), lambda i, j=j: (i, j))
    zprev = lambda j: pl.BlockSpec((8, D_BR), lambda i, j=j: (jnp.maximum(i * (tm // 8) - 1, 0), j))
    za = lambda j: pl.BlockSpec((tm, D_MODEL), lambda i, j=j: (i, Z_MAIN // D_MODEL + j))
    rows512 = pl.BlockSpec((tm, D_BR), lambda i: (i, 0))
    rows = pl.BlockSpec((tm, D_MODEL), lambda i: (i, 0))
    return pl.pallas_call(
        functools.partial(_merge_prompt_kernel, tm=tm, tiles_per_seq=tps),
        grid=(m // tm,),
        in_specs=[rows, rows512, rows512, zb(_CB), zb(_CC), zb(_CX), zprev(_CC), zprev(_CX),
                  za(0), za(1), za(2), _const_spec(conv_w),
                  _const_spec(w_ret_o), _const_spec(w_conv_o), _const_spec(w_fox_o), _const_spec(w_o)],
        out_specs=[rows, pl.BlockSpec((1, 2, D_BR), lambda i: (i // tps, 0, 0))],
        out_shape=[jax.ShapeDtypeStruct((m, D_MODEL), F32),
                   jax.ShapeDtypeStruct((b, 2, D_BR), F32)],
        compiler_params=_cparams(("arbitrary",)),
        name="merge_prompt",
    )(x, o_ret, o_fox, z, z, z, z, z, z, z, z, conv_w, w_ret_o, w_conv_o, w_fox_o, w_o)


def _merge_step_kernel(x_ref, oret_ref, ofox_ref, cb_ref, cc_ref, cx_ref, b0_ref, b1_ref,
                       a1_ref, a2_ref, a3_ref, cw_ref, wr_ref, wc_ref, wf_ref, wo_ref, xo_ref, u_ref):
    u = cc_ref[...] * cx_ref[...]
    y = b0_ref[...] * cw_ref[0:1, :] + b1_ref[...] * cw_ref[1:2, :] + u * cw_ref[2:3, :]
    y_conv = (cb_ref[...] * y).astype(BF16)
    u_ref[...] = u
    xo_ref[...] = _merge_math(x_ref[...], oret_ref[...].astype(BF16), y_conv, ofox_ref[...].astype(BF16),
                              a1_ref[...], a2_ref[...], a3_ref[...], wr_ref, wc_ref, wf_ref, wo_ref)


def _merge_step(x, z, o_ret, o_fox, buf0, buf1, conv_w, w_ret_o, w_conv_o, w_fox_o, w_o):
    m = x.shape[0]
    zb = lambda j: pl.BlockSpec((m, D_BR), lambda i, j=j: (0, j))
    za = lambda j: pl.BlockSpec((m, D_MODEL), lambda i, j=j: (0, Z_MAIN // D_MODEL + j))
    rows512 = pl.BlockSpec((m, D_BR), lambda i: (0, 0))
    rows = pl.BlockSpec((m, D_MODEL), lambda i: (0, 0))
    return pl.pallas_call(
        _merge_step_kernel,
        grid=(1,),
        in_specs=[rows, rows512, rows512, zb(_CB), zb(_CC), zb(_CX), rows512, rows512,
                  za(0), za(1), za(2), _const_spec(conv_w),
                  _const_spec(w_ret_o), _const_spec(w_conv_o), _const_spec(w_fox_o), _const_spec(w_o)],
        out_specs=[rows, rows512],
        out_shape=[jax.ShapeDtypeStruct((m, D_MODEL), F32),
                   jax.ShapeDtypeStruct((m, D_BR), F32)],
        compiler_params=_cparams(("arbitrary",)),
        name="merge_step",
    )(x, o_ret, o_fox, z, z, z, buf0, buf1, z, z, z, conv_w, w_ret_o, w_conv_o, w_fox_o, w_o)


def _ffn_kernel(x_ref, g_ref, wg_ref, wu_ref, wd_ref, o_ref, h_scr):
    f = pl.program_id(1)

    @pl.when(f == 0)
    def _():
        x = x_ref[...]
        h_scr[...] = _rmsnorm(x, g_ref[...]).astype(BF16)
        o_ref[...] = x

    h = h_scr[...]
    act = (_silu(_dot(h, wg_ref[...])) * _dot(h, wu_ref[...])).astype(BF16)
    o_ref[...] += _dot(act, wd_ref[...])


def _ffn(x, g, wg, wu, wd):
    m = x.shape[0]
    d_ff = wg.shape[1]
    tm = min(m, 512)
    tf = d_ff // 2
    rows = pl.BlockSpec((tm, D_MODEL), lambda i, f: (i, 0))
    return pl.pallas_call(
        _ffn_kernel,
        grid=(m // tm, d_ff // tf),
        in_specs=[rows, pl.BlockSpec((1, D_MODEL), lambda i, f: (0, 0)),
                  pl.BlockSpec((D_MODEL, tf), lambda i, f: (0, f)),
                  pl.BlockSpec((D_MODEL, tf), lambda i, f: (0, f)),
                  pl.BlockSpec((tf, D_MODEL), lambda i, f: (f, 0))],
        out_specs=rows,
        out_shape=jax.ShapeDtypeStruct((m, D_MODEL), F32),
        scratch_shapes=[pltpu.VMEM((tm, D_MODEL), BF16)],
        compiler_params=_cparams(("parallel", "arbitrary")),
        name="ffn",
    )(x, g, wg, wu, wd)


def _router_kernel(x_ref, g_ref, wr_ref, comb_ref, sel_ref, cnt_ref):
    @pl.when(pl.program_id(0) == 0)
    def _():
        cnt_ref[...] = jnp.zeros_like(cnt_ref)

    h = _rmsnorm(x_ref[...], g_ref[...])
    logits = _dot_f32(h, wr_ref[...])
    lane = lax.broadcasted_iota(jnp.int32, logits.shape, 1)
    logits = jnp.where(lane < N_EXPERTS, logits, -jnp.inf)
    e = jnp.exp(logits - jnp.max(logits, axis=-1, keepdims=True))
    probs = e / jnp.sum(e, axis=-1, keepdims=True)
    p1 = jnp.max(probs, axis=-1, keepdims=True)
    i1 = jnp.min(jnp.where(probs == p1, lane, LANE), axis=-1, keepdims=True)
    rest = jnp.where(lane == i1, -1.0, probs)
    p2 = jnp.max(rest, axis=-1, keepdims=True)
    i2 = jnp.min(jnp.where(rest == p2, lane, LANE), axis=-1, keepdims=True)
    tot = p1 + p2
    comb_ref[...] = jnp.where(lane == i1, p1 / tot, 0.0) + jnp.where(lane == i2, p2 / tot, 0.0)
    sel = jnp.where((lane == i1) | (lane == i2), 1.0, 0.0)
    sel_ref[...] = sel
    cnt_ref[...] += jnp.sum(sel, axis=0, keepdims=True)


def _router(x, g, wr_pad):
    m = x.shape[0]
    tm = min(m, 512)
    rows = pl.BlockSpec((tm, LANE), lambda i: (i, 0))
    return pl.pallas_call(
        _router_kernel,
        grid=(m // tm,),
        in_specs=[pl.BlockSpec((tm, D_MODEL), lambda i: (i, 0)), _const_spec(g), _const_spec(wr_pad)],
        out_specs=[rows, rows, pl.BlockSpec((1, LANE), lambda i: (0, 0))],
        out_shape=[jax.ShapeDtypeStruct((m, LANE), F32), jax.ShapeDtypeStruct((m, LANE), F32),
                   jax.ShapeDtypeStruct((1, LANE), F32)],
        compiler_params=_cparams(("arbitrary",)),
        name="router",
    )(x, g, wr_pad)


def _positions_kernel(sel_ref, comb_ref, off_ref, tril_ref, pa_ref, pb_ref, wts_ref, carry):
    @pl.when(pl.program_id(0) == 0)
    def _():
        carry[...] = jnp.zeros_like(carry)

    sel = sel_ref[...]
    lane = lax.broadcasted_iota(jnp.int32, sel.shape, 1)
    rank = _dot(tril_ref[...], sel.astype(BF16)) + carry[...]
    pos = off_ref[...] + rank
    picked = sel > 0.0
    ia = jnp.min(jnp.where(picked, lane, LANE), axis=-1, keepdims=True)
    ib = jnp.max(jnp.where(picked, lane, -1), axis=-1, keepdims=True)
    ones = jnp.ones((8, LANE), F32)
    pa_ref[0] = _dot_nt_f32(ones, jnp.where(lane == ia, pos, 0.0))[0:1].astype(jnp.int32)
    pb_ref[0] = _dot_nt_f32(ones, jnp.where(lane == ib, pos, 0.0))[0:1].astype(jnp.int32)
    comb = comb_ref[...]
    wa = jnp.sum(jnp.where(lane == ia, comb, 0.0), axis=-1, keepdims=True)
    wb = jnp.sum(jnp.where(lane == ib, comb, 0.0), axis=-1, keepdims=True)
    wts_ref[...] = jnp.where(lane == 0, wa, 0.0) + jnp.where(lane == 1, wb, 0.0)
    carry[...] += jnp.sum(sel, axis=0, keepdims=True)


def _positions(sel, comb, off):
    m = sel.shape[0]
    tp = min(m, 512)
    tril = jnp.asarray(np.tril(np.ones((tp, tp), np.float32), -1)).astype(BF16)
    rows = pl.BlockSpec((tp, LANE), lambda i: (i, 0))
    prow = pl.BlockSpec((1, 1, tp), lambda i: (i, 0, 0))
    pa, pb, wts = pl.pallas_call(
        _positions_kernel,
        grid=(m // tp,),
        in_specs=[rows, rows, _const_spec(off), _const_spec(tril)],
        out_specs=[prow, prow, rows],
        out_shape=[jax.ShapeDtypeStruct((m // tp, 1, tp), jnp.int32),
                   jax.ShapeDtypeStruct((m // tp, 1, tp), jnp.int32),
                   jax.ShapeDtypeStruct((m, LANE), F32)],
        scratch_shapes=[pltpu.VMEM((1, LANE), F32)],
        compiler_params=_cparams(("arbitrary",)),
        name="moe_positions",
    )(sel, comb, off, tril)
    return pa.reshape(m), pb.reshape(m), wts


def _dispatch_kernel(pa_ref, pb_ref, x_ref, zeros_hbm, xs_hbm, sem, *, ts):
    del zeros_hbm

    def body(t, c):
        src = x_ref.at[pl.ds(t, 1), :]
        pltpu.make_async_copy(src, xs_hbm.at[pl.ds(pa_ref[t], 1), :], sem).start()
        pltpu.make_async_copy(src, xs_hbm.at[pl.ds(pb_ref[t], 1), :], sem).start()
        return c

    lax.fori_loop(0, ts, body, 0, unroll=DMA_UNROLL)
    pltpu.make_async_copy(xs_hbm.at[pl.ds(0, 2 * ts), :], xs_hbm.at[pl.ds(0, 2 * ts), :], sem).wait()


def _dispatch(x, pa, pb, m_pad):
    m = x.shape[0]
    ts = min(m, 1024)
    zeros = jnp.zeros((m_pad, D_MODEL), F32)
    smem = pl.BlockSpec((ts,), lambda i: (i,), memory_space=pltpu.SMEM)
    return pl.pallas_call(
        functools.partial(_dispatch_kernel, ts=ts),
        grid=(m // ts,),
        in_specs=[smem, smem, pl.BlockSpec((ts, D_MODEL), lambda i: (i, 0)), pl.BlockSpec(memory_space=pl.ANY)],
        out_specs=pl.BlockSpec(memory_space=pl.ANY),
        out_shape=jax.ShapeDtypeStruct((m_pad, D_MODEL), F32),
        scratch_shapes=[pltpu.SemaphoreType.DMA(())],
        input_output_aliases={3: 0},
        compiler_params=_cparams(("arbitrary",)),
        name="moe_dispatch",
    )(pa, pb, x, zeros)


def _experts_kernel(te_ref, nu_ref, x_ref, g_ref, wg_ref, wu_ref, wd_ref, o_ref, *, tf, nf):
    del te_ref
    i = pl.program_id(0)

    @pl.when(i < nu_ref[0])
    def _():
        h = _rmsnorm(x_ref[...], g_ref[...]).astype(BF16)
        acc = None
        for f in range(nf):
            sl = slice(f * tf, (f + 1) * tf)
            act = (_silu(_dot(h, wg_ref[0, :, sl])) * _dot(h, wu_ref[0, :, sl])).astype(BF16)
            part = _dot(act, wd_ref[0, sl, :])
            acc = part if acc is None else acc + part
        o_ref[...] = acc

    @pl.when(i >= nu_ref[0])
    def _():
        o_ref[...] = jnp.zeros_like(o_ref)


def _experts(xs, g, tile_expert, n_used, wg, wu, wd, tm):
    m_pad = xs.shape[0]
    d_exp = wg.shape[2]
    nf = MOE_NF
    tf = d_exp // nf
    rows = pl.BlockSpec((tm, D_MODEL), lambda i, te, nu: (i, 0))
    once = pl.Buffered(1)
    grid_spec = pltpu.PrefetchScalarGridSpec(
        num_scalar_prefetch=2,
        grid=(m_pad // tm,),
        in_specs=[rows, pl.BlockSpec((1, D_MODEL), lambda i, te, nu: (0, 0)),
                  pl.BlockSpec((1, D_MODEL, d_exp), lambda i, te, nu: (te[i], 0, 0), pipeline_mode=once),
                  pl.BlockSpec((1, D_MODEL, d_exp), lambda i, te, nu: (te[i], 0, 0), pipeline_mode=once),
                  pl.BlockSpec((1, d_exp, D_MODEL), lambda i, te, nu: (te[i], 0, 0), pipeline_mode=once)],
        out_specs=rows,
    )
    return pl.pallas_call(
        functools.partial(_experts_kernel, tf=tf, nf=nf),
        grid_spec=grid_spec,
        out_shape=jax.ShapeDtypeStruct((m_pad, D_MODEL), F32),
        compiler_params=_cparams(("arbitrary",)),
        name="moe_experts",
    )(tile_expert, n_used, xs, g, wg, wu, wd)


def _combine_kernel(pa_ref, pb_ref, pan_ref, pbn_ref, x_ref, wts_ref, gf_ref, y_hbm, o_ref, bufa, bufb, sem,
                    *, tc, final_norm):
    i = pl.program_id(0)
    slot = i % 2

    def issue(par, pbr, s):
        def body(t, c):
            pltpu.make_async_copy(y_hbm.at[pl.ds(par[t], 1), :], bufa.at[s, pl.ds(t, 1), :], sem.at[s]).start()
            pltpu.make_async_copy(y_hbm.at[pl.ds(pbr[t], 1), :], bufb.at[s, pl.ds(t, 1), :], sem.at[s]).start()
            return c
        lax.fori_loop(0, tc, body, 0, unroll=DMA_UNROLL)

    @pl.when(i == 0)
    def _():
        issue(pa_ref, pb_ref, 0)

    @pl.when(i + 1 < pl.num_programs(0))
    def _():
        issue(pan_ref, pbn_ref, 1 - slot)

    pltpu.make_async_copy(y_hbm.at[pl.ds(0, tc), :], bufa.at[slot], sem.at[slot]).wait()
    pltpu.make_async_copy(y_hbm.at[pl.ds(0, tc), :], bufb.at[slot], sem.at[slot]).wait()
    wts = wts_ref[...]
    out = x_ref[...] + wts[:, 0:1] * bufa[slot] + wts[:, 1:2] * bufb[slot]
    if final_norm:
        out = _rmsnorm(out, gf_ref[...])
    o_ref[...] = out


def _combine(x, y, pa, pb, wts, g_final, final_norm):
    m = x.shape[0]
    tc = min(m, 256)
    n = m // tc
    cur = pl.BlockSpec((tc,), lambda i: (i,), memory_space=pltpu.SMEM)
    nxt = pl.BlockSpec((tc,), lambda i: (jnp.minimum(i + 1, n - 1),), memory_space=pltpu.SMEM)
    rows = pl.BlockSpec((tc, D_MODEL), lambda i: (i, 0))
    return pl.pallas_call(
        functools.partial(_combine_kernel, tc=tc, final_norm=final_norm),
        grid=(n,),
        in_specs=[cur, cur, nxt, nxt, rows, pl.BlockSpec((tc, LANE), lambda i: (i, 0)), _const_spec(g_final),
                  pl.BlockSpec(memory_space=pl.ANY)],
        out_specs=rows,
        out_shape=jax.ShapeDtypeStruct((m, D_MODEL), F32),
        scratch_shapes=[pltpu.VMEM((2, tc, D_MODEL), F32), pltpu.VMEM((2, tc, D_MODEL), F32),
                        pltpu.SemaphoreType.DMA((2,))],
        compiler_params=_cparams(("arbitrary",)),
        name="moe_combine",
    )(pa, pb, pa, pb, x, wts, g_final, y)


def _moe_routed(x, g, wr_pad, wg, wu, wd, g_final, final_norm):
    m = x.shape[0]
    tm = min(MOE_TM, m)
    comb, sel, cnt = _router(x, g, wr_pad)
    counts = cnt[0, :N_EXPERTS].astype(jnp.int32)
    tiles = (counts + tm - 1) // tm
    ends = jnp.cumsum(tiles)
    off = jnp.zeros((1, LANE), F32).at[0, :N_EXPERTS].set(((ends - tiles) * tm).astype(F32))
    n_tiles = (2 * m) // tm + N_EXPERTS
    tile_expert = jnp.minimum(jnp.searchsorted(ends, jnp.arange(n_tiles, dtype=jnp.int32), side="right"),
                              N_EXPERTS - 1).astype(jnp.int32)
    n_used = ends[-1:].astype(jnp.int32)
    pa, pb, wts = _positions(sel, comb, off)
    xs = _dispatch(x, pa, pb, n_tiles * tm)
    y = _experts(xs, g, tile_expert, n_used, wg, wu, wd, tm)
    return _combine(x, y, pa, pb, wts, g_final, final_norm)


def _moe_dense_kernel(x_ref, g_ref, comb_ref, wg_ref, wu_ref, wd_ref, o_ref, h_scr):
    e = pl.program_id(1)
    f = pl.program_id(2)

    @pl.when((e == 0) & (f == 0))
    def _():
        x = x_ref[...]
        h_scr[...] = _rmsnorm(x, g_ref[...]).astype(BF16)
        o_ref[...] = x

    h = h_scr[...]
    comb = comb_ref[...]
    lane = lax.broadcasted_iota(jnp.int32, comb.shape, 1)
    ce = jnp.sum(jnp.where(lane == e, comb, 0.0), axis=-1, keepdims=True)
    act = (_silu(_dot(h, wg_ref[0])) * _dot(h, wu_ref[0])).astype(BF16)
    o_ref[...] += ce * _dot(act, wd_ref[0])


def _moe_dense(x, g, comb, wg, wu, wd):
    m = x.shape[0]
    d_exp = wg.shape[2]
    tm = min(m, 512)
    tf = d_exp // 4
    rows = pl.BlockSpec((tm, D_MODEL), lambda i, e, f: (i, 0))
    return pl.pallas_call(
        _moe_dense_kernel,
        grid=(m // tm, N_EXPERTS, d_exp // tf),
        in_specs=[rows, pl.BlockSpec((1, D_MODEL), lambda i, e, f: (0, 0)),
                  pl.BlockSpec((tm, LANE), lambda i, e, f: (i, 0)),
                  pl.BlockSpec((1, D_MODEL, tf), lambda i, e, f: (e, 0, f)),
                  pl.BlockSpec((1, D_MODEL, tf), lambda i, e, f: (e, 0, f)),
                  pl.BlockSpec((1, tf, D_MODEL), lambda i, e, f: (e, f, 0))],
        out_specs=rows,
        out_shape=jax.ShapeDtypeStruct((m, D_MODEL), F32),
        scratch_shapes=[pltpu.VMEM((tm, D_MODEL), BF16)],
        compiler_params=_cparams(("parallel", "arbitrary", "arbitrary")),
        name="moe_dense",
    )(x, g, comb, wg, wu, wd)


def _final_norm_kernel(x_ref, g_ref, o_ref):
    o_ref[...] = _rmsnorm(x_ref[...], g_ref[...])


def _final_norm(x, g):
    m = x.shape[0]
    tm = min(m, 1024)
    rows = pl.BlockSpec((tm, D_MODEL), lambda i: (i, 0))
    return pl.pallas_call(
        _final_norm_kernel,
        grid=(m // tm,),
        in_specs=[rows, _const_spec(g)],
        out_specs=rows,
        out_shape=jax.ShapeDtypeStruct((m, D_MODEL), F32),
        compiler_params=_cparams(("parallel",)),
        name="final_norm",
    )(x, g)


def _prep_w_in(w):
    main = w[:, :Z_MAIN]
    ff = jnp.pad(w[:, Z_MAIN:Z_MAIN + H], ((0, 0), (0, LANE - H)))
    gates = w[:, Z_MAIN + H:]
    return jnp.concatenate([main, gates], axis=1).astype(BF16), ff.astype(BF16)


def _rope_tables(pos):
    half = HD // 2
    inv = ROPE_BASE ** (-jnp.arange(half, dtype=F32) / half)
    ang = pos.astype(F32)[:, None] * inv[None, :]
    cos = jnp.cos(ang)
    sin = jnp.sin(ang)
    cos512 = jnp.tile(jnp.concatenate([cos, cos], axis=1), (1, H))
    sin512 = jnp.tile(jnp.concatenate([-sin, sin], axis=1), (1, H))
    return cos512, sin512


def kernel(x_prompt, x_sample, state_ret, state_conv, cache_k, cache_v, cache_logf, page_table, g_mix, w_in,
           b_forget, conv_w, w_ret_o, w_conv_o, w_fox_o, w_o, g_ffn, w_ffn_gate, w_ffn_up, w_ffn_down, w_router,
           w_exp_gate, w_exp_up, w_exp_down, g_final):
    b, l, _ = x_prompt.shape
    bd, t, _ = x_sample.shape
    assert t == 1, "decode group carries one new position per sequence"
    depth = w_in.shape[0]
    n_pages = page_table.shape[1]
    n_phys = cache_k.shape[1]

    log_gamma = jnp.log(1.0 - jnp.exp2(-5.0 - jnp.arange(H, dtype=F32)))
    tables = _ret_tables(log_gamma)
    gamma = jnp.exp(log_gamma)
    cos_p, sin_p = _rope_tables(jnp.arange(l))
    cos_s, sin_s = _rope_tables(n_pages * PAGE + jnp.arange(1))
    cache_kt = jnp.transpose(cache_k, (0, 1, 3, 4, 2))
    cache_vt = jnp.transpose(cache_v, (0, 1, 3, 4, 2))
    cache_lft = jnp.transpose(cache_logf, (0, 1, 3, 2))
    g_fin = g_final.reshape(1, D_MODEL)
    z_layers = []

    xp = x_prompt.reshape(b * l, D_MODEL)
    xs = x_sample.reshape(bd, D_MODEL)
    outs = {k: [] for k in ("ret_p", "ret_s", "buf_p", "buf_s", "lfp", "ks", "vs", "lfs")}
    for layer in range(depth):
        last = layer == depth - 1
        w_in_l, w_ff_l = _prep_w_in(w_in[layer])
        g_l = g_mix[layer].reshape(1, D_MODEL)
        b_pad = jnp.pad(b_forget[layer], (0, LANE - H)).reshape(1, LANE)
        wr_o = w_ret_o[layer].astype(BF16)
        wc_o = w_conv_o[layer].astype(BF16)
        wf_o = w_fox_o[layer].astype(BF16)
        wo = w_o[layer].astype(BF16)
        cw = conv_w[layer]

        z, zb, ff = _inproj(xp, g_l, w_in_l, w_ff_l)
        o_ret, st = _ret_prompt(z, cos_p, sin_p, tables, b, l)
        lf, bias = _forget_prompt(ff, b_pad, b, l)
        o_fox = _fox_prompt(zb, bias.reshape(b, H // 2, 2, l), b, l)
        xp, buf = _merge_prompt(xp, z, o_ret, o_fox, cw, wr_o, wc_o, wf_o, wo, b, l)
        outs["ret_p"].append(st)
        outs["buf_p"].append(buf)
        z_layers.append(z)
        outs["lfp"].append(lf.reshape(b, l, H))

        zs, _, ffs = _inproj(xs, g_l, w_in_l, w_ff_l)
        seg = lambda j: zs[:, j * D_BR:(j + 1) * D_BR]
        o_ret_s, st_s = _ret_step(zs, state_ret[layer], cos_s, sin_s, gamma)
        lf_s = _forget_step(ffs, b_pad)[:, :H]
        o_fox_s = _fox_decode(page_table, seg(_FQ).reshape(bd, H, HD, 1), seg(_FK).reshape(bd, H, HD, 1),
                              seg(_FV).reshape(bd, H, HD, 1), lf_s.reshape(bd, H, 1),
                              cache_kt, cache_vt, cache_lft, layer).reshape(bd, D_BR)
        xs, u_s = _merge_step(xs, zs, o_ret_s, o_fox_s, state_conv[layer, :, 0], state_conv[layer, :, 1],
                              cw, wr_o, wc_o, wf_o, wo)
        outs["ret_s"].append(st_s)
        outs["buf_s"].append(jnp.stack([state_conv[layer, :, 1], u_s], axis=1))
        outs["ks"].append(seg(_FK).reshape(bd, 1, H, HD))
        outs["vs"].append(seg(_FV).reshape(bd, 1, H, HD))
        outs["lfs"].append(lf_s.reshape(bd, 1, H))

        g_f = g_ffn[layer].reshape(1, D_MODEL)
        mi = layer // 2
        if layer % 2 == 0:
            wg = w_ffn_gate[mi].astype(BF16)
            wu = w_ffn_up[mi].astype(BF16)
            wd = w_ffn_down[mi].astype(BF16)
            xp = _ffn(xp, g_f, wg, wu, wd)
            xs = _ffn(xs, g_f, wg, wu, wd)
        else:
            wr_pad = jnp.pad(w_router[mi], ((0, 0), (0, LANE - N_EXPERTS)))
            wg = w_exp_gate[mi].astype(BF16)
            wu = w_exp_up[mi].astype(BF16)
            wd = w_exp_down[mi].astype(BF16)
            xp = _moe_routed(xp, g_f, wr_pad, wg, wu, wd, g_fin, last)
            xs = _moe_dense(xs, g_f, _router(xs, g_f, wr_pad)[0], wg, wu, wd)
        if last and layer % 2 == 0:
            xp = _final_norm(xp, g_fin)

    y_prompt = xp.reshape(b, l, D_MODEL)
    y_sample = _final_norm(xs, g_fin).reshape(bd, 1, D_MODEL)
    st = lambda k: jnp.stack(outs[k])
    k_prompt, v_prompt = (jnp.transpose(a.reshape(depth, b, H, HD, l), (0, 1, 4, 2, 3))
                          for a in _kv_transposed(z_layers, b, l))
    return (y_prompt, y_sample, st("ret_p"), st("ret_s"), st("buf_p"), st("buf_s"),
            k_prompt, v_prompt, st("lfp"), st("ks"), st("vs"), st("lfs"))
```

```python
import functools

import jax
import jax.numpy as jnp
import numpy as np
from jax import lax
from jax.experimental import pallas as pl
from jax.experimental.pallas import tpu as pltpu

F32 = jnp.float32
BF16 = jnp.bfloat16

D_MODEL = 1024
H = 8
HD = 64
D_BR = H * HD
RET_CHUNK = 128
PAGE = 128
DEC_PAGES = 32
N_EXPERTS = 8
ROPE_BASE = 10000.0
EPS = 1e-6
LOG2E = 1.4426950408889634
LANE = 128
VMEM_LIMIT = 56 * 1024 * 1024

_RQ, _RK, _RV, _RG, _CB, _CC, _CX, _FQ, _FK, _FV = range(10)
Z_MAIN = 10 * D_BR
Z_COLS = Z_MAIN + 3 * D_MODEL
Z_TN = 2048
FOX_TQ = 1024
FOX_TK = 1024
FOX_TD = 512
RET_ROWS = 512
MOE_TM = 1024
MOE_NF = 7
DMA_UNROLL = 4


def _cparams(sem):
    return pltpu.CompilerParams(dimension_semantics=sem, vmem_limit_bytes=VMEM_LIMIT)


def _rmsnorm(x, g):
    return x * lax.rsqrt(jnp.mean(x * x, axis=-1, keepdims=True) + EPS) * g


def _dot(a, b):
    return jnp.dot(a, b, preferred_element_type=F32)


def _dot_nt(a, b):
    return lax.dot_general(a, b, (((1,), (1,)), ((), ())), preferred_element_type=F32)


def _dot_tn(a, b):
    return lax.dot_general(a, b, (((0,), (0,)), ((), ())), preferred_element_type=F32)


def _dot_f32(a, b):
    return jnp.dot(a, b, preferred_element_type=F32, precision=lax.Precision.HIGHEST)


def _dot_nt_f32(a, b):
    return lax.dot_general(a, b, (((1,), (1,)), ((), ())), preferred_element_type=F32,
                           precision=lax.Precision.HIGHEST)


def _sigmoid(x):
    return 1.0 / (1.0 + jnp.exp(-x))


def _silu(x):
    return x * _sigmoid(x)


def _log_sigmoid(x):
    return jnp.minimum(x, 0.0) - jnp.log(1.0 + jnp.exp(-jnp.abs(x)))


def _const_spec(a):
    n = a.ndim
    return pl.BlockSpec(a.shape, lambda *_, n=n: (0,) * n)


def _inproj_kernel(x_ref, g_ref, w_ref, wf_ref, z_ref, zb_ref, ff_ref, h_scr):
    j = pl.program_id(1)

    @pl.when(j == 0)
    def _():
        h = _rmsnorm(x_ref[...], g_ref[...]).astype(BF16)
        h_scr[...] = h
        ff_ref[...] = _dot(h, wf_ref[...])

    z = _dot(h_scr[...], w_ref[...])
    z_ref[...] = z

    @pl.when(j == 1)
    def _():
        zb_ref[:, 0:D_BR] = (z[:, 3 * D_BR:] * (HD ** -0.5 * LOG2E)).astype(BF16)

    @pl.when(j == 2)
    def _():
        zb_ref[:, D_BR:] = z[:, :2 * D_BR].astype(BF16)


def _inproj(x, g, w, w_ff):
    m = x.shape[0]
    tm = min(m, 1024)
    assert Z_COLS // Z_TN == 4 and _FQ * D_BR == Z_TN + 3 * D_BR and _FK * D_BR == 2 * Z_TN
    return pl.pallas_call(
        _inproj_kernel,
        grid=(m // tm, Z_COLS // Z_TN),
        in_specs=[
            pl.BlockSpec((tm, D_MODEL), lambda i, j: (i, 0)),
            pl.BlockSpec((1, D_MODEL), lambda i, j: (0, 0)),
            pl.BlockSpec((D_MODEL, Z_TN), lambda i, j: (0, j)),
            pl.BlockSpec((D_MODEL, LANE), lambda i, j: (0, 0)),
        ],
        out_specs=[pl.BlockSpec((tm, Z_TN), lambda i, j: (i, j)),
                   pl.BlockSpec((tm, 3 * D_BR), lambda i, j: (i, 0)),
                   pl.BlockSpec((tm, LANE), lambda i, j: (i, 0))],
        out_shape=[jax.ShapeDtypeStruct((m, Z_COLS), F32),
                   jax.ShapeDtypeStruct((m, 3 * D_BR), BF16),
                   jax.ShapeDtypeStruct((m, LANE), F32)],
        scratch_shapes=[pltpu.VMEM((tm, D_MODEL), BF16)],
        compiler_params=_cparams(("parallel", "arbitrary")),
        name="inproj",
    )(x, g, w, w_ff)


def _rope(x, cos, sin_signed):
    lane = lax.broadcasted_iota(jnp.int32, x.shape, 1)
    first = (lane % HD) < (HD // 2)
    n = x.shape[1]
    partner = jnp.where(first, pltpu.roll(x, n - HD // 2, 1), pltpu.roll(x, HD // 2, 1))
    return x * cos + partner * sin_signed


def _ret_prompt_kernel(rq_ref, rk_ref, rv_ref, rg_ref, cos_ref, sin_ref, dec_ref, qd_ref, kd_ref, sd_ref,
                       o_ref, st_ref, s_scr, *, chunks):
    c = pl.program_id(1)

    @pl.when(c == 0)
    def _():
        s_scr[...] = jnp.zeros_like(s_scr)

    low = lax.broadcasted_iota(jnp.int32, (RET_CHUNK, LANE), 1) < HD
    row = lax.broadcasted_iota(jnp.int32, (LANE, LANE), 0)
    col = lax.broadcasted_iota(jnp.int32, (LANE, LANE), 1)
    same_head = (row < HD) == (col < HD)
    states = [s_scr[p] for p in range(H // 2)]
    for ch in range(chunks):
        rs = slice(ch * RET_CHUNK, (ch + 1) * RET_CHUNK)
        cos = cos_ref[rs, :]
        sin = sin_ref[rs, :]
        q = _rope(rq_ref[rs, :], cos, sin)
        k = _rope(rk_ref[rs, :], cos, sin) * (HD ** -0.5)
        v = rv_ref[rs, :]
        for p in range(H // 2):
            sl = slice(p * LANE, (p + 1) * LANE)
            qp = q[:, sl]
            kt = k[:, sl].T
            ktb = kt.astype(BF16)
            vp = v[:, sl].astype(BF16)
            inner = []
            for hh in range(2):
                qm = jnp.where(low if hh == 0 else jnp.logical_not(low), qp, 0.0).astype(BF16)
                sc = _dot(qm, ktb) * dec_ref[2 * p + hh]
                inner.append(_dot(sc.astype(BF16), vp))
            s_old = states[p]
            cross = _dot(qp.astype(BF16), s_old.astype(BF16)) * qd_ref[:, sl]
            o = jnp.where(low, inner[0], inner[1]) + cross
            kdec = (kt * kd_ref[sl, :]).astype(BF16)
            states[p] = sd_ref[p] * s_old + jnp.where(same_head, _dot(kdec, vp), 0.0)
            oo = o * o
            ms = jnp.where(low, jnp.sum(jnp.where(low, oo, 0.0), axis=-1, keepdims=True),
                           jnp.sum(jnp.where(low, 0.0, oo), axis=-1, keepdims=True)) * (1.0 / HD)
            o_ref[rs, sl] = (o * lax.rsqrt(ms + EPS) * _silu(rg_ref[rs, sl])).astype(BF16)
    for p in range(H // 2):
        s_scr[p] = states[p]

    @pl.when(c == pl.num_programs(1) - 1)
    def _():
        for p in range(H // 2):
            st_ref[0, 2 * p] = states[p][:HD, :HD]
            st_ref[0, 2 * p + 1] = states[p][HD:, HD:]


def _ret_tables(log_gamma):
    i = jnp.arange(RET_CHUNK, dtype=F32)
    diff = i[:, None] - i[None, :]
    decay = jnp.where(diff >= 0, jnp.exp(log_gamma[:, None, None] * jnp.maximum(diff, 0.0)), 0.0)
    q_decay = jnp.exp(log_gamma[:, None] * (i + 1.0)[None, :]).T
    k_decay = jnp.exp(log_gamma[:, None] * (RET_CHUNK - 1.0 - i)[None, :]).T
    s_decay = jnp.exp(log_gamma * RET_CHUNK)
    qd = jnp.repeat(q_decay, HD, axis=1)
    kd = jnp.repeat(k_decay, HD, axis=1).T
    sd =jnp.broadcast_to(jnp.repeat(s_decay, HD).reshape(H // 2, LANE, 1), (H // 2, LANE, LANE))
    return decay, qd, kd, sd


def _ret_prompt(z, cos, sin, tables, b, l):
    decay, qd, kd, sd = tables
    tr = min(l, RET_ROWS)
    nc = l // tr
    blk = lambda j: pl.BlockSpec((tr, D_BR), lambda bi, c, j=j: (bi * nc + c, j))
    tab = pl.BlockSpec((tr, D_BR), lambda bi, c: (c, 0))
    return pl.pallas_call(
        functools.partial(_ret_prompt_kernel, chunks=tr // RET_CHUNK),
        grid=(b, nc),
        in_specs=[blk(_RQ), blk(_RK), blk(_RV), blk(_RG), tab, tab,
                  _const_spec(decay), _const_spec(qd), _const_spec(kd), _const_spec(sd)],
        out_specs=[pl.BlockSpec((tr, D_BR), lambda bi, c: (bi * nc + c, 0)),
                   pl.BlockSpec((1, H, HD, HD), lambda bi, c: (bi, 0, 0, 0))],
        out_shape=[jax.ShapeDtypeStruct((b * l, D_BR), BF16),
                   jax.ShapeDtypeStruct((b, H, HD, HD), F32)],
        scratch_shapes=[pltpu.VMEM((H // 2, LANE, LANE), F32)],
        compiler_params=_cparams(("parallel", "arbitrary")),
        name="ret_prompt",
    )(z, z, z, z, cos, sin, decay, qd, kd, sd)


def _ret_step_kernel(q_ref, k_ref, v_ref, g_ref, s_ref, cos_ref, sin_ref, gam_ref, grow_ref, o_ref, sn_ref):
    cos = cos_ref[...]
    sin = sin_ref[...]
    q = _rope(q_ref[0], cos, sin).astype(BF16)
    k = (_rope(k_ref[0], cos, sin) * (HD ** -0.5)).astype(BF16)
    own = (lax.broadcasted_iota(jnp.int32, (H, D_BR), 1) // HD) == lax.broadcasted_iota(jnp.int32, (H, D_BR), 0)
    q_bd = jnp.where(own, jnp.broadcast_to(q.astype(F32), (H, D_BR)), 0.0)
    k_bd = jnp.where(own, jnp.broadcast_to(k.astype(F32), (H, D_BR)), 0.0)
    v = v_ref[0].astype(BF16)
    sc = jnp.sum(q_bd * k.astype(F32), axis=-1, keepdims=True)
    inner = sc.astype(BF16).astype(F32) * v.astype(F32)
    s_old = s_ref[0]
    cross = _dot(q_bd.astype(BF16), s_old.astype(BF16)) * gam_ref[...]
    o = inner + cross
    o = o * lax.rsqrt(jnp.mean(o * o, axis=-1, keepdims=True) + EPS)
    o_ref[0] = o * _silu(g_ref[0])
    sn_ref[0] = grow_ref[...] * s_old + _dot_tn(k_bd.astype(BF16), v)


def _ret_step(zs, state, cos512, sin512, gamma):
    bd = zs.shape[0]
    seg = lambda j: zs[:, j * D_BR:(j + 1) * D_BR]
    s2 = state.reshape(bd, D_BR, HD)
    gam = gamma.reshape(H, 1)
    grow = jnp.broadcast_to(jnp.repeat(gamma, HD)[:, None], (D_BR, HD))
    flat = pl.BlockSpec((1, 1, D_BR), lambda i: (i, 0, 0))
    r3 = pl.BlockSpec((1, H, HD), lambda i: (i, 0, 0))
    s3 = pl.BlockSpec((1, D_BR, HD), lambda i: (i, 0, 0))
    o, sn = pl.pallas_call(
        _ret_step_kernel,
        grid=(bd,),
        in_specs=[flat, flat, r3, r3, s3,
                  _const_spec(cos512), _const_spec(sin512), _const_spec(gam), _const_spec(grow)],
        out_specs=[r3, s3],
        out_shape=[jax.ShapeDtypeStruct((bd, H, HD), F32),
                   jax.ShapeDtypeStruct((bd, D_BR, HD), F32)],
        compiler_params=_cparams(("parallel",)),
        name="ret_step",
    )(seg(_RQ).reshape(bd, 1, D_BR), seg(_RK).reshape(bd, 1, D_BR), seg(_RV).reshape(bd, H, HD),
      seg(_RG).reshape(bd, H, HD), s2, cos512, sin512, gam, grow)
    return o.reshape(bd, D_BR), sn.reshape(bd, H, HD, HD)


def _forget_prompt_kernel(ff_ref, b_ref, tri_ref, lf_ref, bias_ref, carry):
    @pl.when(pl.program_id(1) == 0)
    def _():
        carry[...] = jnp.zeros_like(carry)

    lf = _log_sigmoid(ff_ref[...] + b_ref[...])
    lf_ref[...] = lf[:, :H]
    c = _dot_f32(tri_ref[...], lf) + carry[...]
    bias_ref[0] = c.T[:H, :] * (-LOG2E)
    carry[...] = c[-1:, :]


def _forget_prompt(ff, b_pad, b, l):
    tc = min(l, 512)
    nt = l // tc
    tri = jnp.asarray(np.tril(np.ones((tc, tc), np.float32)))
    return pl.pallas_call(
        _forget_prompt_kernel,
        grid=(b, nt),
        in_specs=[pl.BlockSpec((tc, LANE), lambda bi, t: (bi * nt + t, 0)),
                  _const_spec(b_pad), _const_spec(tri)],
        out_specs=[pl.BlockSpec((tc, H), lambda bi, t: (bi * nt + t, 0)),
                   pl.BlockSpec((1, H, tc), lambda bi, t: (bi, 0, t))],
        out_shape=[jax.ShapeDtypeStruct((b * l, H), F32),
                   jax.ShapeDtypeStruct((b, H, l), F32)],
        scratch_shapes=[pltpu.VMEM((1, LANE), F32)],
        compiler_params=_cparams(("parallel", "arbitrary")),
        name="forget_prompt",
    )(ff, b_pad, tri)


def _forget_step_kernel(ff_ref, b_ref, lf_ref):
    lf_ref[...] = _log_sigmoid(ff_ref[...] + b_ref[...])


def _forget_step(ff, b_pad):
    m = ff.shape[0]
    full = pl.BlockSpec((m, LANE), lambda i: (0, 0))
    return pl.pallas_call(
        _forget_step_kernel,
        grid=(1,),
        in_specs=[full, _const_spec(b_pad)],
        out_specs=full,
        out_shape=jax.ShapeDtypeStruct((m, LANE), F32),
        name="forget_step",
    )(ff, b_pad)


def _fox_prompt_kernel(q_ref, k_ref, v_ref, bias_ref, o_ref, *, tq, tk, td):
    qi = pl.program_id(2)
    zero = jnp.zeros((), BF16)
    one = jnp.ones((), BF16)
    lowq = lax.broadcasted_iota(jnp.int32, (tq, LANE), 1) < HD
    q = q_ref[...]
    qm = (jnp.where(lowq, q, zero), jnp.where(lowq, zero, q))

    def step(ks, width, r0, carry, masked):
        rows = tq - r0
        lowk = lax.broadcasted_iota(jnp.int32, (width, LANE), 1) < HD
        kb = k_ref[pl.ds(ks, width), :]
        vb = v_ref[pl.ds(ks, width), :]
        vh = (jnp.where(lowk, vb, one), jnp.where(lowk, one, vb))
        new = []
        for hh in range(2):
            m_all, acc_all = carry[hh]
            m, acc = m_all[r0:], acc_all[r0:]
            s = _dot_nt(qm[hh][r0:], kb) + bias_ref[0, 0, pl.ds(hh, 1), pl.ds(ks, width)]
            if masked:
                qpos = qi * tq + r0 + lax.broadcasted_iota(jnp.int32, (rows, width), 0)
                kpos = ks + lax.broadcasted_iota(jnp.int32, (rows, width), 1)
                s = jnp.where(kpos <= qpos, s, -jnp.inf)
            m_new = jnp.maximum(m, jnp.max(s, axis=-1, keepdims=True))
            p = jnp.exp2(s - m_new)
            acc = jnp.exp2(m - m_new) * acc + _dot(p.astype(BF16), vh[hh])
            if r0:
                m_new = jnp.concatenate([m_all[:r0], m_new], axis=0)
                acc = jnp.concatenate([acc_all[:r0], acc], axis=0)
            new.append((m_new, acc))
        return tuple(new)

    init = tuple((jnp.full((tq, 1), -jnp.inf, F32), jnp.zeros((tq, LANE), F32)) for _ in range(2))
    n_full = qi * (tq // tk)
    carry = lax.fori_loop(0, n_full, lambda j, c: step(pl.multiple_of(j * tk, tk), tk, 0, c, False), init)
    for d in range(tq // td):
        carry = step(pl.multiple_of(qi * tq + d * td, td), td, d * td, carry, True)
    (_, acc0), (_, acc1) = carry
    out0 = acc0 / acc0[:, HD:HD + 1]
    out1 = acc1 / acc1[:, 0:1]
    o_ref[...] = jnp.where(lowq, out0, out1).astype(BF16)


def _fox_prompt(zb, bias, b, l):
    tq = min(l, FOX_TQ)
    tk = min(tq, FOX_TK)
    td = min(tk, FOX_TD)
    nq = l // tq
    npair = H // 2
    qspec = pl.BlockSpec((tq, LANE), lambda bi, p, qi: (bi * nq + qi, p))
    kspec = pl.BlockSpec((l, LANE), lambda bi, p, qi: (bi, npair + p))
    vspec = pl.BlockSpec((l, LANE), lambda bi, p, qi: (bi, 2 * npair + p))
    return pl.pallas_call(
        functools.partial(_fox_prompt_kernel, tq=tq, tk=tk, td=td),
        grid=(b, npair, nq),
        in_specs=[qspec, kspec, vspec,
                  pl.BlockSpec((1, 1, 2, l), lambda bi, p, qi: (bi, p, 0, 0))],
        out_specs=qspec,
        out_shape=jax.ShapeDtypeStruct((b * l, D_BR), BF16),
        compiler_params=_cparams(("parallel", "parallel", "arbitrary")),
        name="fox_prompt",
    )(zb, zb, zb, bias)


def _fox_decode_kernel(pt_ref, q_ref, kn_ref, vn_ref, lfn_ref, suf_ref, *rest, g):
    del pt_ref
    k_refs = rest[:g]
    v_refs = rest[g:2 * g]
    lf_refs = rest[2 * g:3 * g]
    o_ref = rest[3 * g]
    m_scr, l_scr, acc_scr, cy_scr = rest[3 * g + 1:]
    j = pl.program_id(1)
    lane = lax.broadcasted_iota(jnp.int32, (HD, PAGE), 1)
    qcol = [q_ref[0, h] * (HD ** -0.5 * LOG2E) for h in range(H)]

    @pl.when(j == 0)
    def _():
        m_scr[...] = jnp.concatenate(
            [jnp.sum(qcol[h] * kn_ref[0, h], axis=0, keepdims=True) for h in range(H)], axis=0)
        l_scr[...] = jnp.ones_like(l_scr)
        for h in range(H):
            acc_scr[h] = jnp.where(lane == 0, vn_ref[0, h], 0.0)
        cy_scr[...] = lfn_ref[0] * LOG2E

    lf = jnp.concatenate([lf_refs[t][0, 0] for t in range(g)], axis=0) * LOG2E
    suf = _dot_f32(lf, suf_ref[...])
    tot = jnp.sum(lf, axis=1, keepdims=True)
    cy = cy_scr[...]
    rows = []
    for t in range(g):
        s_t = jnp.concatenate(
            [jnp.sum(k_refs[t][0, 0, h] * qcol[h], axis=0, keepdims=True) for h in range(H)], axis=0)
        rows.append(s_t + suf[t * H:(t + 1) * H] + cy)
        cy = cy + tot[t * H:(t + 1) * H]
    cy_scr[...] = cy
    m_old = m_scr[...]
    m_new = m_old
    for t in range(g):
        m_new = jnp.maximum(m_new, jnp.max(rows[t], axis=1, keepdims=True))
    alpha = jnp.exp2(m_old - m_new)
    ps = [jnp.exp2(rows[t] - m_new) for t in range(g)]
    l_new = alpha * l_scr[...]
    for t in range(g):
        l_new = l_new + jnp.sum(ps[t], axis=1, keepdims=True)
    l_scr[...] = l_new
    m_scr[...] = m_new
    for h in range(H):
        acc = alpha[h:h + 1, :] * acc_scr[h]
        for t in range(g):
            acc = acc + v_refs[t][0, 0, h] * ps[t][h:h + 1, :]
        acc_scr[h] = acc

    @pl.when(j == pl.num_programs(1) - 1)
    def _():
        l = l_scr[...]
        for h in range(H):
            o_ref[0, h] = jnp.sum(acc_scr[h], axis=1, keepdims=True) / l[h:h + 1, :]


def _fox_decode(page_table, qcol, kncol, vncol, lfn, cache_kt, cache_vt, cache_lft, layer):
    bd, n_pages = page_table.shape
    g = DEC_PAGES if n_pages % DEC_PAGES == 0 else 1
    steps = n_pages // g
    jj = np.arange(PAGE)
    suf = jnp.asarray((jj[:, None] > jj[None, :]).astype(np.float32))

    def page(nd):
        def spec(t):
            return lambda b, j, pt, t=t: (layer, pt[b, n_pages - 1 - (j * g + t)]) + (0,) * nd
        return spec

    c4 = pl.BlockSpec((1, H, HD, 1), lambda b, j, pt: (b, 0, 0, 0))
    in_specs = [c4, c4, c4, pl.BlockSpec((1, H, 1), lambda b, j, pt: (b, 0, 0)),
                pl.BlockSpec((PAGE, PAGE), lambda b, j, pt: (0, 0))]
    in_specs += [pl.BlockSpec((1, 1, H, HD, PAGE), page(3)(t)) for t in range(g)]
    in_specs += [pl.BlockSpec((1, 1, H, HD, PAGE), page(3)(t)) for t in range(g)]
    in_specs += [pl.BlockSpec((1, 1, H, PAGE), page(2)(t)) for t in range(g)]
    grid_spec = pltpu.PrefetchScalarGridSpec(
        num_scalar_prefetch=1,
        grid=(bd, steps),
        in_specs=in_specs,
        out_specs=c4,
        scratch_shapes=[pltpu.VMEM((H, 1), F32), pltpu.VMEM((H, 1), F32),
                        pltpu.VMEM((H, HD, PAGE), F32), pltpu.VMEM((H, 1), F32)],
    )
    return pl.pallas_call(
        functools.partial(_fox_decode_kernel, g=g),
        grid_spec=grid_spec,
        out_shape=jax.ShapeDtypeStruct((bd, H, HD, 1), F32),
        compiler_params=_cparams(("parallel", "arbitrary")),
        name="fox_decode",
    )(page_table, qcol, kncol, vncol, lfn, suf, *([cache_kt] * g), *([cache_vt] * g), *([cache_lft] * g))


def _kv_transposed_kernel(*refs):
    depth = len(refs) // 2 - 1
    kt_ref, vt_ref = refs[-2], refs[-1]
    for layer in range(depth):
        kt_ref[layer, 0] = refs[2 * layer][...].T
        vt_ref[layer, 0] = refs[2 * layer + 1][...].T


def _kv_transposed(zs, b, l):
    depth = len(zs)
    m = zs[0].shape[0]
    tm = min(l, 1024)
    tps = l // tm
    zspec = lambda j: pl.BlockSpec((tm, D_BR), lambda i, j=j: (i, j))
    ospec = pl.BlockSpec((depth, 1, D_BR, tm), lambda i: (0, i // tps, 0, i % tps))
    shape = jax.ShapeDtypeStruct((depth, b, D_BR, l), F32)
    return pl.pallas_call(
        _kv_transposed_kernel,
        grid=(m // tm,),
        in_specs=[zspec(_FK), zspec(_FV)] * depth,
        out_specs=[ospec, ospec],
        out_shape=[shape, shape],
        compiler_params=_cparams(("parallel",)),
        name="kv_transposed",
    )(*[z for z in zs for _ in range(2)])


def _merge_math(x, o_ret, y_conv, o_fox, a1, a2, a3, wr_ref, wc_ref, wf_ref, wo_ref):
    merged = (_sigmoid(a1) * _dot(o_ret, wr_ref[...])
              + _sigmoid(a2) * _dot(y_conv, wc_ref[...])
              + _sigmoid(a3) * _dot(o_fox, wf_ref[...]))
    return x + _dot(merged.astype(BF16), wo_ref[...])


def _merge_prompt_kernel(x_ref, oret_ref, ofox_ref, cb_ref, cc_ref, cx_ref, ccp_ref, cxp_ref,
                         a1_ref, a2_ref, a3_ref, cw_ref, wr_ref, wc_ref, wf_ref, wo_ref,
                         xo_ref, buf_ref, *, tm, tiles_per_seq):
    i = pl.program_id(0)
    u = cc_ref[...] * cx_ref[...]
    prev = ccp_ref[...] * cxp_ref[...]
    prev = jnp.where(i % tiles_per_seq == 0, 0.0, prev)
    p1 = prev[7:8, :]
    p2 = prev[6:7, :]
    row = lax.broadcasted_iota(jnp.int32, u.shape, 0)
    u1 = jnp.where(row >= 1, pltpu.roll(u, 1, 0), p1)
    u2 = jnp.where(row >= 2, pltpu.roll(u, 2, 0), jnp.where(row == 1, p1, p2))
    y = u2 * cw_ref[0:1, :] + u1 * cw_ref[1:2, :] + u * cw_ref[2:3, :]
    y_conv = (cb_ref[...] * y).astype(BF16)
    buf_ref[0] = u[tm - 2:, :]
    xo_ref[...] = _merge_math(x_ref[...], oret_ref[...], y_conv, ofox_ref[...],
                              a1_ref[...], a2_ref[...], a3_ref[...], wr_ref, wc_ref, wf_ref, wo_ref)


def _merge_prompt(x, z, o_ret, o_fox, conv_w, w_ret_o, w_conv_o, w_fox_o, w_o, b, l):
    m = b * l
    tm = min(l, 512)
    tps = l // tm
    zb = lambda j: pl.BlockSpec((tm, D_BR), lambda i, j=j: (i, j))
    zprev = lambda j: pl.BlockSpec((8, D_BR), lambda i, j=j: (jnp.maximum(i * (tm // 8) - 1, 0), j))
    za = lambda j: pl.BlockSpec((tm, D_MODEL), lambda i, j=j: (i, Z_MAIN // D_MODEL + j))
    rows512 = pl.BlockSpec((tm, D_BR), lambda i: (i, 0))
    rows = pl.BlockSpec((tm, D_MODEL), lambda i: (i, 0))
    return pl.pallas_call(
        functools.partial(_merge_prompt_kernel, tm=tm, tiles_per_seq=tps),
        grid=(m // tm,),
        in_specs=[rows, rows512, rows512, zb(_CB), zb(_CC), zb(_CX), zprev(_CC), zprev(_CX),
                  za(0), za(1), za(2), _const_spec(conv_w),
                  _const_spec(w_ret_o), _const_spec(w_conv_o), _const_spec(w_fox_o), _const_spec(w_o)],
        out_specs=[rows, pl.BlockSpec((1, 2, D_BR), lambda i: (i // tps, 0, 0))],
        out_shape=[jax.ShapeDtypeStruct((m, D_MODEL), F32),
                   jax.ShapeDtypeStruct((b, 2, D_BR), F32)],
        compiler_params=_cparams(("arbitrary",)),
        name="merge_prompt",
    )(x, o_ret, o_fox, z, z, z, z, z, z, z, z, conv_w, w_ret_o, w_conv_o, w_fox_o, w_o)


def _merge_step_kernel(x_ref, oret_ref, ofox_ref, cb_ref, cc_ref, cx_ref, b0_ref, b1_ref,
                       a1_ref, a2_ref, a3_ref, cw_ref, wr_ref, wc_ref, wf_ref, wo_ref, xo_ref, u_ref):
    u = cc_ref[...] * cx_ref[...]
    y = b0_ref[...] * cw_ref[0:1, :] + b1_ref[...] * cw_ref[1:2, :] + u * cw_ref[2:3, :]
    y_conv = (cb_ref[...] * y).astype(BF16)
    u_ref[...] = u
    xo_ref[...] = _merge_math(x_ref[...], oret_ref[...].astype(BF16), y_conv, ofox_ref[...].astype(BF16),
                              a1_ref[...], a2_ref[...], a3_ref[...], wr_ref, wc_ref, wf_ref, wo_ref)


def _merge_step(x, z, o_ret, o_fox, buf0, buf1, conv_w, w_ret_o, w_conv_o, w_fox_o, w_o):
    m = x.shape[0]
    zb = lambda j: pl.BlockSpec((m, D_BR), lambda i, j=j: (0, j))
    za = lambda j: pl.BlockSpec((m, D_MODEL), lambda i, j=j: (0, Z_MAIN // D_MODEL + j))
    rows512 = pl.BlockSpec((m, D_BR), lambda i: (0, 0))
    rows = pl.BlockSpec((m, D_MODEL), lambda i: (0, 0))
    return pl.pallas_call(
        _merge_step_kernel,
        grid=(1,),
        in_specs=[rows, rows512, rows512, zb(_CB), zb(_CC), zb(_CX), rows512, rows512,
                  za(0), za(1), za(2), _const_spec(conv_w),
                  _const_spec(w_ret_o), _const_spec(w_conv_o), _const_spec(w_fox_o), _const_spec(w_o)],
        out_specs=[rows, rows512],
        out_shape=[jax.ShapeDtypeStruct((m, D_MODEL), F32),
                   jax.ShapeDtypeStruct((m, D_BR), F32)],
        compiler_params=_cparams(("arbitrary",)),
        name="merge_step",
    )(x, o_ret, o_fox, z, z, z, buf0, buf1, z, z, z, conv_w, w_ret_o, w_conv_o, w_fox_o, w_o)


def _ffn_kernel(x_ref, g_ref, wg_ref, wu_ref, wd_ref, o_ref, h_scr):
    f = pl.program_id(1)

    @pl.when(f == 0)
    def _():
        x = x_ref[...]
        h_scr[...] = _rmsnorm(x, g_ref[...]).astype(BF16)
        o_ref[...] = x

    h = h_scr[...]
    act = (_silu(_dot(h, wg_ref[...])) * _dot(h, wu_ref[...])).astype(BF16)
    o_ref[...] += _dot(act, wd_ref[...])


def _ffn(x, g, wg, wu, wd):
    m = x.shape[0]
    d_ff = wg.shape[1]
    tm = min(m, 512)
    tf = d_ff // 2
    rows = pl.BlockSpec((tm, D_MODEL), lambda i, f: (i, 0))
    return pl.pallas_call(
        _ffn_kernel,
        grid=(m // tm, d_ff // tf),
        in_specs=[rows, pl.BlockSpec((1, D_MODEL), lambda i, f: (0, 0)),
                  pl.BlockSpec((D_MODEL, tf), lambda i, f: (0, f)),
                  pl.BlockSpec((D_MODEL, tf), lambda i, f: (0, f)),
                  pl.BlockSpec((tf, D_MODEL), lambda i, f: (f, 0))],
        out_specs=rows,
        out_shape=jax.ShapeDtypeStruct((m, D_MODEL), F32),
        scratch_shapes=[pltpu.VMEM((tm, D_MODEL), BF16)],
        compiler_params=_cparams(("parallel", "arbitrary")),
        name="ffn",
    )(x, g, wg, wu, wd)


def _router_kernel(x_ref, g_ref, wr_ref, comb_ref, sel_ref, cnt_ref):
    @pl.when(pl.program_id(0) == 0)
    def _():
        cnt_ref[...] = jnp.zeros_like(cnt_ref)

    h = _rmsnorm(x_ref[...], g_ref[...])
    logits = _dot_f32(h, wr_ref[...])
    lane = lax.broadcasted_iota(jnp.int32, logits.shape, 1)
    logits = jnp.where(lane < N_EXPERTS, logits, -jnp.inf)
    e = jnp.exp(logits - jnp.max(logits, axis=-1, keepdims=True))
    probs = e / jnp.sum(e, axis=-1, keepdims=True)
    p1 = jnp.max(probs, axis=-1, keepdims=True)
    i1 = jnp.min(jnp.where(probs == p1, lane, LANE), axis=-1, keepdims=True)
    rest = jnp.where(lane == i1, -1.0, probs)
    p2 = jnp.max(rest, axis=-1, keepdims=True)
    i2 = jnp.min(jnp.where(rest == p2, lane, LANE), axis=-1, keepdims=True)
    tot = p1 + p2
    comb_ref[...] = jnp.where(lane == i1, p1 / tot, 0.0) + jnp.where(lane == i2, p2 / tot, 0.0)
    sel = jnp.where((lane == i1) | (lane == i2), 1.0, 0.0)
    sel_ref[...] = sel
    cnt_ref[...] += jnp.sum(sel, axis=0, keepdims=True)


def _router(x, g, wr_pad):
    m = x.shape[0]
    tm = min(m, 512)
    rows = pl.BlockSpec((tm, LANE), lambda i: (i, 0))
    return pl.pallas_call(
        _router_kernel,
        grid=(m // tm,),
        in_specs=[pl.BlockSpec((tm, D_MODEL), lambda i: (i, 0)), _const_spec(g), _const_spec(wr_pad)],
        out_specs=[rows, rows, pl.BlockSpec((1, LANE), lambda i: (0, 0))],
        out_shape=[jax.ShapeDtypeStruct((m, LANE), F32), jax.ShapeDtypeStruct((m, LANE), F32),
                   jax.ShapeDtypeStruct((1, LANE), F32)],
        compiler_params=_cparams(("arbitrary",)),
        name="router",
    )(x, g, wr_pad)


def _positions_kernel(sel_ref, comb_ref, off_ref, tril_ref, pa_ref, pb_ref, wts_ref, carry):
    @pl.when(pl.program_id(0) == 0)
    def _():
        carry[...] = jnp.zeros_like(carry)

    sel = sel_ref[...]
    lane = lax.broadcasted_iota(jnp.int32, sel.shape, 1)
    rank = _dot(tril_ref[...], sel.astype(BF16)) + carry[...]
    pos = off_ref[...] + rank
    picked = sel > 0.0
    ia = jnp.min(jnp.where(picked, lane, LANE), axis=-1, keepdims=True)
    ib = jnp.max(jnp.where(picked, lane, -1), axis=-1, keepdims=True)
    ones = jnp.ones((8, LANE), F32)
    pa_ref[0] = _dot_nt_f32(ones, jnp.where(lane == ia, pos, 0.0))[0:1].astype(jnp.int32)
    pb_ref[0] = _dot_nt_f32(ones, jnp.where(lane == ib, pos, 0.0))[0:1].astype(jnp.int32)
    comb = comb_ref[...]
    wa = jnp.sum(jnp.where(lane == ia, comb, 0.0), axis=-1, keepdims=True)
    wb = jnp.sum(jnp.where(lane == ib, comb, 0.0), axis=-1, keepdims=True)
    wts_ref[...] = jnp.where(lane == 0, wa, 0.0) + jnp.where(lane == 1, wb, 0.0)
    carry[...] += jnp.sum(sel, axis=0, keepdims=True)


def _positions(sel, comb, off):
    m = sel.shape[0]
    tp = min(m, 512)
    tril = jnp.asarray(np.tril(np.ones((tp, tp), np.float32), -1)).astype(BF16)
    rows = pl.BlockSpec((tp, LANE), lambda i: (i, 0))
    prow = pl.BlockSpec((1, 1, tp), lambda i: (i, 0, 0))
    pa, pb, wts = pl.pallas_call(
        _positions_kernel,
        grid=(m // tp,),
        in_specs=[rows, rows, _const_spec(off), _const_spec(tril)],
        out_specs=[prow, prow, rows],
        out_shape=[jax.ShapeDtypeStruct((m // tp, 1, tp), jnp.int32),
                   jax.ShapeDtypeStruct((m // tp, 1, tp), jnp.int32),
                   jax.ShapeDtypeStruct((m, LANE), F32)],
        scratch_shapes=[pltpu.VMEM((1, LANE), F32)],
        compiler_params=_cparams(("arbitrary",)),
        name="moe_positions",
    )(sel, comb, off, tril)
    return pa.reshape(m), pb.reshape(m), wts


def _dispatch_kernel(pa_ref, pb_ref, x_ref, zeros_hbm, xs_hbm, sem, *, ts):
    del zeros_hbm

    def body(t, c):
        src = x_ref.at[pl.ds(t, 1), :]
        pltpu.make_async_copy(src, xs_hbm.at[pl.ds(pa_ref[t], 1), :], sem).start()
        pltpu.make_async_copy(src, xs_hbm.at[pl.ds(pb_ref[t], 1), :], sem).start()
        return c

    lax.fori_loop(0, ts, body, 0, unroll=DMA_UNROLL)
    pltpu.make_async_copy(xs_hbm.at[pl.ds(0, 2 * ts), :], xs_hbm.at[pl.ds(0, 2 * ts), :], sem).wait()


def _dispatch(x, pa, pb, m_pad):
    m = x.shape[0]
    ts = min(m, 1024)
    zeros = jnp.zeros((m_pad, D_MODEL), F32)
    smem = pl.BlockSpec((ts,), lambda i: (i,), memory_space=pltpu.SMEM)
    return pl.pallas_call(
        functools.partial(_dispatch_kernel, ts=ts),
        grid=(m // ts,),
        in_specs=[smem, smem, pl.BlockSpec((ts, D_MODEL), lambda i: (i, 0)), pl.BlockSpec(memory_space=pl.ANY)],
        out_specs=pl.BlockSpec(memory_space=pl.ANY),
        out_shape=jax.ShapeDtypeStruct((m_pad, D_MODEL), F32),
        scratch_shapes=[pltpu.SemaphoreType.DMA(())],
        input_output_aliases={3: 0},
        compiler_params=_cparams(("arbitrary",)),
        name="moe_dispatch",
    )(pa, pb, x, zeros)


def _experts_kernel(te_ref, nu_ref, x_ref, g_ref, wg_ref, wu_ref, wd_ref, o_ref, *, tf, nf):
    del te_ref
    i = pl.program_id(0)

    @pl.when(i < nu_ref[0])
    def _():
        h = _rmsnorm(x_ref[...], g_ref[...]).astype(BF16)
        acc = None
        for f in range(nf):
            sl = slice(f * tf, (f + 1) * tf)
            act = (_silu(_dot(h, wg_ref[0, :, sl])) * _dot(h, wu_ref[0, :, sl])).astype(BF16)
            part = _dot(act, wd_ref[0, sl, :])
            acc = part if acc is None else acc + part
        o_ref[...] = acc

    @pl.when(i >= nu_ref[0])
    def _():
        o_ref[...] = jnp.zeros_like(o_ref)


def _experts(xs, g, tile_expert, n_used, wg, wu, wd, tm):
    m_pad = xs.shape[0]
    d_exp = wg.shape[2]
    nf = MOE_NF
    tf = d_exp // nf
    rows = pl.BlockSpec((tm, D_MODEL), lambda i, te, nu: (i, 0))
    once = pl.Buffered(1)
    grid_spec = pltpu.PrefetchScalarGridSpec(
        num_scalar_prefetch=2,
        grid=(m_pad // tm,),
        in_specs=[rows, pl.BlockSpec((1, D_MODEL), lambda i, te, nu: (0, 0)),
                  pl.BlockSpec((1, D_MODEL, d_exp), lambda i, te, nu: (te[i], 0, 0), pipeline_mode=once),
                  pl.BlockSpec((1, D_MODEL, d_exp), lambda i, te, nu: (te[i], 0, 0), pipeline_mode=once),
                  pl.BlockSpec((1, d_exp, D_MODEL), lambda i, te, nu: (te[i], 0, 0), pipeline_mode=once)],
        out_specs=rows,
    )
    return pl.pallas_call(
        functools.partial(_experts_kernel, tf=tf, nf=nf),
        grid_spec=grid_spec,
        out_shape=jax.ShapeDtypeStruct((m_pad, D_MODEL), F32),
        compiler_params=_cparams(("arbitrary",)),
        name="moe_experts",
    )(tile_expert, n_used, xs, g, wg, wu, wd)


def _combine_kernel(pa_ref, pb_ref, pan_ref, pbn_ref, x_ref, wts_ref, gf_ref, y_hbm, o_ref, bufa, bufb, sem,
                    *, tc, final_norm):
    i = pl.program_id(0)
    slot = i % 2

    def issue(par, pbr, s):
        def body(t, c):
            pltpu.make_async_copy(y_hbm.at[pl.ds(par[t], 1), :], bufa.at[s, pl.ds(t, 1), :], sem.at[s]).start()
            pltpu.make_async_copy(y_hbm.at[pl.ds(pbr[t], 1), :], bufb.at[s, pl.ds(t, 1), :], sem.at[s]).start()
            return c
        lax.fori_loop(0, tc, body, 0, unroll=DMA_UNROLL)

    @pl.when(i == 0)
    def _():
        issue(pa_ref, pb_ref, 0)

    @pl.when(i + 1 < pl.num_programs(0))
    def _():
        issue(pan_ref, pbn_ref, 1 - slot)

    pltpu.make_async_copy(y_hbm.at[pl.ds(0, tc), :], bufa.at[slot], sem.at[slot]).wait()
    pltpu.make_async_copy(y_hbm.at[pl.ds(0, tc), :], bufb.at[slot], sem.at[slot]).wait()
    wts = wts_ref[...]
    out = x_ref[...] + wts[:, 0:1] * bufa[slot] + wts[:, 1:2] * bufb[slot]
    if final_norm:
        out = _rmsnorm(out, gf_ref[...])
    o_ref[...] = out


def _combine(x, y, pa, pb, wts, g_final, final_norm):
    m = x.shape[0]
    tc = min(m, 256)
    n = m // tc
    cur = pl.BlockSpec((tc,), lambda i: (i,), memory_space=pltpu.SMEM)
    nxt = pl.BlockSpec((tc,), lambda i: (jnp.minimum(i + 1, n - 1),), memory_space=pltpu.SMEM)
    rows = pl.BlockSpec((tc, D_MODEL), lambda i: (i, 0))
    return pl.pallas_call(
        functools.partial(_combine_kernel, tc=tc, final_norm=final_norm),
        grid=(n,),
        in_specs=[cur, cur, nxt, nxt, rows, pl.BlockSpec((tc, LANE), lambda i: (i, 0)), _const_spec(g_final),
                  pl.BlockSpec(memory_space=pl.ANY)],
        out_specs=rows,
        out_shape=jax.ShapeDtypeStruct((m, D_MODEL), F32),
        scratch_shapes=[pltpu.VMEM((2, tc, D_MODEL), F32), pltpu.VMEM((2, tc, D_MODEL), F32),
                        pltpu.SemaphoreType.DMA((2,))],
        compiler_params=_cparams(("arbitrary",)),
        name="moe_combine",
    )(pa, pb, pa, pb, x, wts, g_final, y)


def _moe_routed(x, g, wr_pad, wg, wu, wd, g_final, final_norm):
    m = x.shape[0]
    tm = min(MOE_TM, m)
    comb, sel, cnt = _router(x, g, wr_pad)
    counts = cnt[0, :N_EXPERTS].astype(jnp.int32)
    tiles = (counts + tm - 1) // tm
    ends = jnp.cumsum(tiles)
    off = jnp.zeros((1, LANE), F32).at[0, :N_EXPERTS].set(((ends - tiles) * tm).astype(F32))
    n_tiles = (2 * m) // tm + N_EXPERTS
    tile_expert = jnp.minimum(jnp.searchsorted(ends, jnp.arange(n_tiles, dtype=jnp.int32), side="right"),
                              N_EXPERTS - 1).astype(jnp.int32)
    n_used = ends[-1:].astype(jnp.int32)
    pa, pb, wts = _positions(sel, comb, off)
    xs = _dispatch(x, pa, pb, n_tiles * tm)
    y = _experts(xs, g, tile_expert, n_used, wg, wu, wd, tm)
    return _combine(x, y, pa, pb, wts, g_final, final_norm)


def _moe_dense_kernel(x_ref, g_ref, comb_ref, wg_ref, wu_ref, wd_ref, o_ref, h_scr):
    e = pl.program_id(1)
    f = pl.program_id(2)

    @pl.when((e == 0) & (f == 0))
    def _():
        x = x_ref[...]
        h_scr[...] = _rmsnorm(x, g_ref[...]).astype(BF16)
        o_ref[...] = x

    h = h_scr[...]
    comb = comb_ref[...]
    lane = lax.broadcasted_iota(jnp.int32, comb.shape, 1)
    ce = jnp.sum(jnp.where(lane == e, comb, 0.0), axis=-1, keepdims=True)
    act = (_silu(_dot(h, wg_ref[0])) * _dot(h, wu_ref[0])).astype(BF16)
    o_ref[...] += ce * _dot(act, wd_ref[0])


def _moe_dense(x, g, comb, wg, wu, wd):
    m = x.shape[0]
    d_exp = wg.shape[2]
    tm = min(m, 512)
    tf = d_exp // 4
    rows = pl.BlockSpec((tm, D_MODEL), lambda i, e, f: (i, 0))
    return pl.pallas_call(
        _moe_dense_kernel,
        grid=(m // tm, N_EXPERTS, d_exp // tf),
        in_specs=[rows, pl.BlockSpec((1, D_MODEL), lambda i, e, f: (0, 0)),
                  pl.BlockSpec((tm, LANE), lambda i, e, f: (i, 0)),
                  pl.BlockSpec((1, D_MODEL, tf), lambda i, e, f: (e, 0, f)),
                  pl.BlockSpec((1, D_MODEL, tf), lambda i, e, f: (e, 0, f)),
                  pl.BlockSpec((1, tf, D_MODEL), lambda i, e, f: (e, f, 0))],
        out_specs=rows,
        out_shape=jax.ShapeDtypeStruct((m, D_MODEL), F32),
        scratch_shapes=[pltpu.VMEM((tm, D_MODEL), BF16)],
        compiler_params=_cparams(("parallel", "arbitrary", "arbitrary")),
        name="moe_dense",
    )(x, g, comb, wg, wu, wd)


def _final_norm_kernel(x_ref, g_ref, o_ref):
    o_ref[...] = _rmsnorm(x_ref[...], g_ref[...])


def _final_norm(x, g):
    m = x.shape[0]
    tm = min(m, 1024)
    rows = pl.BlockSpec((tm, D_MODEL), lambda i: (i, 0))
    return pl.pallas_call(
        _final_norm_kernel,
        grid=(m // tm,),
        in_specs=[rows, _const_spec(g)],
        out_specs=rows,
        out_shape=jax.ShapeDtypeStruct((m, D_MODEL), F32),
        compiler_params=_cparams(("parallel",)),
        name="final_norm",
    )(x, g)


def _prep_w_in(w):
    main = w[:, :Z_MAIN]
    ff = jnp.pad(w[:, Z_MAIN:Z_MAIN + H], ((0, 0), (0, LANE - H)))
    gates = w[:, Z_MAIN + H:]
    return jnp.concatenate([main, gates], axis=1).astype(BF16), ff.astype(BF16)


def _rope_tables(pos):
    half = HD // 2
    inv = ROPE_BASE ** (-jnp.arange(half, dtype=F32) / half)
    ang = pos.astype(F32)[:, None] * inv[None, :]
    cos = jnp.cos(ang)
    sin = jnp.sin(ang)
    cos512 = jnp.tile(jnp.concatenate([cos, cos], axis=1), (1, H))
    sin512 = jnp.tile(jnp.concatenate([-sin, sin], axis=1), (1, H))
    return cos512, sin512


def kernel(x_prompt, x_sample, state_ret, state_conv, cache_k, cache_v, cache_logf, page_table, g_mix, w_in,
           b_forget, conv_w, w_ret_o, w_conv_o, w_fox_o, w_o, g_ffn, w_ffn_gate, w_ffn_up, w_ffn_down, w_router,
           w_exp_gate, w_exp_up, w_exp_down, g_final):
    b, l, _ = x_prompt.shape
    bd, t, _ = x_sample.shape
    assert t == 1, "decode group carries one new position per sequence"
    depth = w_in.shape[0]
    n_pages = page_table.shape[1]
    n_phys = cache_k.shape[1]

    log_gamma = jnp.log(1.0 - jnp.exp2(-5.0 - jnp.arange(H, dtype=F32)))
    tables = _ret_tables(log_gamma)
    gamma = jnp.exp(log_gamma)
    cos_p, sin_p = _rope_tables(jnp.arange(l))
    cos_s, sin_s = _rope_tables(n_pages * PAGE + jnp.arange(1))
    cache_kt = jnp.transpose(cache_k, (0, 1, 3, 4, 2))
    cache_vt = jnp.transpose(cache_v, (0, 1, 3, 4, 2))
    cache_lft = jnp.transpose(cache_logf, (0, 1, 3, 2))
    g_fin = g_final.reshape(1, D_MODEL)
    z_layers = []

    xp = x_prompt.reshape(b * l, D_MODEL)
    xs = x_sample.reshape(bd, D_MODEL)
    outs = {k: [] for k in ("ret_p", "ret_s", "buf_p", "buf_s", "lfp", "ks", "vs", "lfs")}
    for layer in range(depth):
        last = layer == depth - 1
        w_in_l, w_ff_l = _prep_w_in(w_in[layer])
        g_l = g_mix[layer].reshape(1, D_MODEL)
        b_pad = jnp.pad(b_forget[layer], (0, LANE - H)).reshape(1, LANE)
        wr_o = w_ret_o[layer].astype(BF16)
        wc_o = w_conv_o[layer].astype(BF16)
        wf_o = w_fox_o[layer].astype(BF16)
        wo = w_o[layer].astype(BF16)
        cw = conv_w[layer]

        z, zb, ff = _inproj(xp, g_l, w_in_l, w_ff_l)
        o_ret, st = _ret_prompt(z, cos_p, sin_p, tables, b, l)
        lf, bias = _forget_prompt(ff, b_pad, b, l)
        o_fox = _fox_prompt(zb, bias.reshape(b, H // 2, 2, l), b, l)
        xp, buf = _merge_prompt(xp, z, o_ret, o_fox, cw, wr_o, wc_o, wf_o, wo, b, l)
        outs["ret_p"].append(st)
        outs["buf_p"].append(buf)
        z_layers.append(z)
        outs["lfp"].append(lf.reshape(b, l, H))

        zs, _, ffs = _inproj(xs, g_l, w_in_l, w_ff_l)
        seg = lambda j: zs[:, j * D_BR:(j + 1) * D_BR]
        o_ret_s, st_s = _ret_step(zs, state_ret[layer], cos_s, sin_s, gamma)
        lf_s = _forget_step(ffs, b_pad)[:, :H]
        o_fox_s = _fox_decode(page_table, seg(_FQ).reshape(bd, H, HD, 1), seg(_FK).reshape(bd, H, HD, 1),
                              seg(_FV).reshape(bd, H, HD, 1), lf_s.reshape(bd, H, 1),
                              cache_kt, cache_vt, cache_lft, layer).reshape(bd, D_BR)
        xs, u_s = _merge_step(xs, zs, o_ret_s, o_fox_s, state_conv[layer, :, 0], state_conv[layer, :, 1],
                              cw, wr_o, wc_o, wf_o, wo)
        outs["ret_s"].append(st_s)
        outs["buf_s"].append(jnp.stack([state_conv[layer, :, 1], u_s], axis=1))
        outs["ks"].append(seg(_FK).reshape(bd, 1, H, HD))
        outs["vs"].append(seg(_FV).reshape(bd, 1, H, HD))
        outs["lfs"].append(lf_s.reshape(bd, 1, H))

        g_f = g_ffn[layer].reshape(1, D_MODEL)
        mi = layer // 2
        if layer % 2 == 0:
            wg = w_ffn_gate[mi].astype(BF16)
            wu = w_ffn_up[mi].astype(BF16)
            wd = w_ffn_down[mi].astype(BF16)
            xp = _ffn(xp, g_f, wg, wu, wd)
            xs = _ffn(xs, g_f, wg, wu, wd)
        else:
            wr_pad = jnp.pad(w_router[mi], ((0, 0), (0, LANE - N_EXPERTS)))
            wg = w_exp_gate[mi].astype(BF16)
            wu = w_exp_up[mi].astype(BF16)
            wd = w_exp_down[mi].astype(BF16)
            xp = _moe_routed(xp, g_f, wr_pad, wg, wu, wd, g_fin, last)
            xs = _moe_dense(xs, g_f, _router(xs, g_f, wr_pad)[0], wg, wu, wd)
        if last and layer % 2 == 0:
            xp = _final_norm(xp, g_fin)

    y_prompt = xp.reshape(b, l, D_MODEL)
    y_sample = _final_norm(xs, g_fin).reshape(bd, 1, D_MODEL)
    st = lambda k: jnp.stack(outs[k])
    k_prompt, v_prompt = (jnp.transpose(a.reshape(depth, b, H, HD, l), (0, 1, 4, 2, 3))
                          for a in _kv_transposed(z_layers, b, l))
    return (y_prompt, y_sample, st("ret_p"), st("ret_s"), st("buf_p"), st("buf_s"),
            k_prompt, v_prompt, st("lfp"), st("ks"), st("vs"), st("lfs"))
```
